```python
import math
import jax, jax.numpy as jnp
from jax import lax
import numpy as np

D_MODEL = 1024
BATCH = 8
SEQ = 8192
DEPTH = 1

ATTN_HEADS = D_MODEL // 256
ATTN_HEAD_DIM = 64
ATTN_V_DIM = 2 * ATTN_HEAD_DIM
ATTN_QK_WIDTH = ATTN_HEADS * 2 * ATTN_HEAD_DIM
ATTN_WIDTH = ATTN_HEADS * ATTN_V_DIM
Q_BLOCK = 128
REC_HEADS = D_MODEL // 256
REC_KEY_DIM = 128
REC_VAL_DIM = 128
REC_KEY_WIDTH = REC_HEADS * REC_KEY_DIM
REC_WIDTH = REC_HEADS * REC_VAL_DIM
REC_CHUNK = 64
IN_SPLITS = (ATTN_QK_WIDTH, ATTN_QK_WIDTH, ATTN_WIDTH, REC_KEY_WIDTH, REC_KEY_WIDTH, REC_WIDTH, REC_WIDTH)
IN_WIDTH = sum(IN_SPLITS)
MIX_WIDTH = ATTN_WIDTH + REC_WIDTH
NUM_BUCKETS = 32
MAX_DISTANCE = 128
N_GROUPS = 4
EXPERTS_PER_GROUP = 8
N_EXPERTS = N_GROUPS * EXPERTS_PER_GROUP
EXPERT_FF = D_MODEL // 2
TOP_K = 2
EXPERT_BLOCK = 128
EPS = 1e-6

kernel_name = 'hybrid_diffattn_hgrn2_hier_moe'


def rmsnorm(x, g):
    xf = x.astype(jnp.float32)
    y = xf * lax.rsqrt(jnp.mean(xf * xf, axis=-1, keepdims=True) + EPS)
    return (y * g.astype(jnp.float32)).astype(x.dtype)


def t5_bucket(rel):
    n = jnp.maximum(rel, 0)
    max_exact = NUM_BUCKETS // 2
    nf = jnp.maximum(n, 1).astype(jnp.float32)
    large = max_exact + (jnp.log(nf / max_exact) / math.log(MAX_DISTANCE / max_exact)
                         * (NUM_BUCKETS - max_exact)).astype(jnp.int32)
    large = jnp.minimum(large, NUM_BUCKETS - 1)
    return jnp.where(n < max_exact, n, large)


def lambda_init(layer):
    return 0.8 - 0.6 * math.exp(-0.3 * layer)


def diff_attention(q, k, v, lq1, lk1, lq2, lk2, subln_g, rel_table, layer):
    B, S, _ = q.shape
    nblk = S // Q_BLOCK
    q = q.reshape(B, S, ATTN_HEADS, 2, ATTN_HEAD_DIM).transpose(0, 2, 3, 1, 4)
    k = k.reshape(B, S, ATTN_HEADS, 2, ATTN_HEAD_DIM).transpose(0, 2, 3, 1, 4)
    v = v.reshape(B, S, ATTN_HEADS, ATTN_V_DIM).transpose(0, 2, 1, 3)
    lam_init = lambda_init(layer)
    f32 = jnp.float32
    lam = (jnp.exp(jnp.sum(lq1.astype(f32) * lk1.astype(f32)))
           - jnp.exp(jnp.sum(lq2.astype(f32) * lk2.astype(f32))) + lam_init)
    scale = ATTN_HEAD_DIM ** -0.5
    q_blocks = jnp.moveaxis(q.reshape(B, ATTN_HEADS, 2, nblk, Q_BLOCK, ATTN_HEAD_DIM), 3, 0)
    k_pos = jnp.arange(S)

    def block(args):
        qb, start = args
        q_pos = start + jnp.arange(Q_BLOCK)
        rel = q_pos[:, None] - k_pos[None, :]
        bias = jnp.transpose(rel_table[t5_bucket(rel)], (2, 0, 1)).astype(f32)
        logits = jnp.einsum('bhcqd,bhckd->bhcqk', qb, k).astype(f32) * scale + bias[None, :, None]
        logits = jnp.where(rel >= 0, logits, -jnp.inf)
        p = jax.nn.softmax(logits, axis=-1)
        a = p[:, :, 0] - lam * p[:, :, 1]
        return jnp.einsum('bhqk,bhkv->bhqv', a.astype(v.dtype), v)

    o = lax.map(block, (q_blocks, jnp.arange(nblk) * Q_BLOCK))
    o = jnp.moveaxis(o, 0, 2).reshape(B, ATTN_HEADS, S, ATTN_V_DIM).transpose(0, 2, 1, 3)
    o = rmsnorm(o, subln_g) * (1.0 - lam_init)
    return o.reshape(B, S, ATTN_WIDTH)


def hgrn2(q, f, i, g, lb, norm_g):
    B, S, _ = q.shape
    f32 = jnp.float32
    n = S // REC_CHUNK
    qf = jax.nn.silu(q.astype(f32))
    fr = f.astype(f32)
    lb = lb.astype(f32)
    log_f = jnp.log(lb + (1.0 - lb) * jax.nn.sigmoid(fr))
    kf = (1.0 - lb) * jax.nn.sigmoid(-fr)
    vf = i.astype(f32)

    def to_chunks(t, d):
        return t.reshape(B, n, REC_CHUNK, REC_HEADS, d).transpose(1, 0, 3, 2, 4)

    qc = to_chunks(qf, REC_KEY_DIM)
    kc = to_chunks(kf, REC_KEY_DIM)
    fc = to_chunks(log_f, REC_KEY_DIM)
    vc = to_chunks(vf, REC_VAL_DIM)
    tri = jnp.tril(jnp.ones((REC_CHUNK, REC_CHUNK), bool))

    def step(state, inp):
        qt, kt, lft, vt = inp
        b = jnp.cumsum(lft, axis=2)
        o_inter = jnp.einsum('bhtk,bhkv->bhtv', qt * jnp.exp(b), state)
        rel_decay = jnp.exp(jnp.where(tri[:, :, None],
                                      b[:, :, :, None, :] - b[:, :, None, :, :], -jnp.inf))
        scores = jnp.einsum('bhtk,bhsk,bhtsk->bhts', qt, kt, rel_decay)
        o = o_inter + jnp.einsum('bhts,bhsv->bhtv', scores, vt)
        b_last = b[:, :, -1:, :]
        state = (jnp.exp(b_last[:, :, 0, :])[..., None] * state
                 + jnp.einsum('bhsk,bhsv->bhkv', kt * jnp.exp(b_last - b), vt))
        return state, o

    s0 = jnp.zeros((B, REC_HEADS, REC_KEY_DIM, REC_VAL_DIM), f32)
    _, o = lax.scan(step, s0, (qc, kc, fc, vc))
    o = o.transpose(1, 0, 3, 2, 4).reshape(B, S, REC_HEADS, REC_VAL_DIM)
    o = rmsnorm(o, norm_g.reshape(REC_HEADS, REC_VAL_DIM))
    o = o * jax.nn.silu(g.astype(f32)).reshape(B, S, REC_HEADS, REC_VAL_DIM)
    return o.reshape(B, S, REC_WIDTH).astype(g.dtype)


def hier_moe(h, w_group, b_group, w_expert, b_expert, w1, w3, w2):
    B, S, D = h.shape
    N = B * S
    A = N * TOP_K
    f32 = jnp.float32
    hf = h.reshape(N, D)
    g_prob = jax.nn.softmax((hf @ w_group + b_group).astype(f32), axis=-1)
    g_gate, g_idx = lax.top_k(g_prob, 1)
    e_logits = (hf @ w_expert + b_expert).astype(f32).reshape(N, N_GROUPS, EXPERTS_PER_GROUP)
    e_logits = jnp.take_along_axis(e_logits, g_idx[:, :, None], axis=1)[:, 0]
    e_top, e_idx = lax.top_k(jax.nn.softmax(e_logits, axis=-1), TOP_K)
    weights = g_gate * e_top / jnp.sum(e_top, axis=-1, keepdims=True)
    expert_id = (g_idx * EXPERTS_PER_GROUP + e_idx).reshape(A)
    order = jnp.argsort(expert_id)
    sorted_e = expert_id[order]
    tok = order // TOP_K
    w_sorted = weights.reshape(A)[order]
    counts = jnp.bincount(expert_id, length=N_EXPERTS)
    start = jnp.cumsum(counts) - counts
    padded = (counts + EXPERT_BLOCK - 1) // EXPERT_BLOCK * EXPERT_BLOCK
    padded_end = jnp.cumsum(padded)
    padded_start = padded_end - padded
    dest = padded_start[sorted_e] + jnp.arange(A) - start[sorted_e]
    n_blocks = -(-A // EXPERT_BLOCK) + N_EXPERTS
    rows = jnp.zeros((n_blocks * EXPERT_BLOCK, D), h.dtype).at[dest].set(hf[tok])
    block_e = jnp.minimum(jnp.searchsorted(padded_end, jnp.arange(n_blocks) * EXPERT_BLOCK, side='right'),
                          N_EXPERTS - 1)

    def expert_block(args):
        xb, e = args
        return (jax.nn.silu(xb @ w1[e]) * (xb @ w3[e])) @ w2[e]

    ys = lax.map(expert_block, (rows.reshape(n_blocks, EXPERT_BLOCK, D), block_e)).reshape(-1, D)
    y = ys[dest] * w_sorted[:, None].astype(h.dtype)
    out = jnp.zeros((N, D), h.dtype).at[tok].add(y)
    return out.reshape(B, S, D)


def _normal(k, shape, s):
    return jax.random.normal(k, shape, jnp.float32) * s


def setup_inputs(seed: int = 0) -> dict:
    key = jax.random.key(seed)
    ks = jax.random.split(key, 24)
    D = D_MODEL
    return {
        'x': _normal(ks[0], (BATCH, SEQ, D), 1.0),
        'c': _normal(ks[1], (BATCH, D), 1.0),
        'w_ada': _normal(ks[2], (DEPTH, D, 6 * D), D ** -0.5),
        'b_ada': _normal(ks[3], (DEPTH, 6 * D), 0.02),
        'norm1_g': 1.0 + _normal(ks[4], (DEPTH, D), 0.02),
        'norm2_g': 1.0 + _normal(ks[5], (DEPTH, D), 0.02),
        'w_in': _normal(ks[6], (DEPTH, D, IN_WIDTH), D ** -0.5),
        'attn_lambda_q1': _normal(ks[7], (DEPTH, ATTN_HEAD_DIM), 0.1),
        'attn_lambda_k1': _normal(ks[8], (DEPTH, ATTN_HEAD_DIM), 0.1),
        'attn_lambda_q2': _normal(ks[9], (DEPTH, ATTN_HEAD_DIM), 0.1),
        'attn_lambda_k2': _normal(ks[10], (DEPTH, ATTN_HEAD_DIM), 0.1),
        'attn_subln_g': 1.0 + _normal(ks[11], (DEPTH, ATTN_V_DIM), 0.02),
        'rel_bias_table': _normal(ks[12], (NUM_BUCKETS, ATTN_HEADS), 0.5),
        'rec_lower_bound': _normal(ks[13], (DEPTH + 1, REC_KEY_WIDTH), 0.1),
        'rec_norm_g': 1.0 + _normal(ks[14], (DEPTH, REC_WIDTH), 0.02),
        'w_out': _normal(ks[15], (DEPTH, MIX_WIDTH, D), MIX_WIDTH ** -0.5),
        'w_group': _normal(ks[16], (DEPTH, D, N_GROUPS), D ** -0.5),
        'b_group': _normal(ks[17], (DEPTH, N_GROUPS), 0.01),
        'w_expert': _normal(ks[18], (DEPTH, D, N_EXPERTS), D ** -0.5),
        'b_expert': _normal(ks[19], (DEPTH, N_EXPERTS), 0.01),
        'w1': _normal(ks[20], (DEPTH, N_EXPERTS, D, EXPERT_FF), D ** -0.5),
        'w3': _normal(ks[21], (DEPTH, N_EXPERTS, D, EXPERT_FF), D ** -0.5),
        'w2': _normal(ks[22], (DEPTH, N_EXPERTS, EXPERT_FF, D), EXPERT_FF ** -0.5),
        'final_norm_g': 1.0 + _normal(ks[23], (D,), 0.02),
    }


def reference(x, c, w_ada, b_ada, norm1_g, norm2_g, w_in, attn_lambda_q1, attn_lambda_k1,
              attn_lambda_q2, attn_lambda_k2, attn_subln_g, rel_bias_table, rec_lower_bound,
              rec_norm_g, w_out, w_group, b_group, w_expert, b_expert, w1, w3, w2, final_norm_g):
    lb_all = jnp.cumsum(jax.nn.softmax(rec_lower_bound.astype(jnp.float32), axis=0), axis=0)
    c_act = jax.nn.silu(c)
    split_points = []
    acc = 0
    for w in IN_SPLITS[:-1]:
        acc += w
        split_points.append(acc)
    for l in range(DEPTH):
        mod = c_act @ w_ada[l] + b_ada[l]
        sh1, sc1, g1, sh2, sc2, g2 = jnp.split(mod, 6, axis=-1)
        h = rmsnorm(x, norm1_g[l]) * (1.0 + sc1[:, None]) + sh1[:, None]
        proj = h @ w_in[l]
        qa, ka, va, qr, fr, ir, gr = jnp.split(proj, split_points, axis=-1)
        ya = diff_attention(qa, ka, va, attn_lambda_q1[l], attn_lambda_k1[l], attn_lambda_q2[l],
                            attn_lambda_k2[l], attn_subln_g[l], rel_bias_table, l)
        yr = hgrn2(qr, fr, ir, gr, lb_all[l], rec_norm_g[l])
        mix = jnp.concatenate([ya, yr], axis=-1) @ w_out[l]
        x = x + g1[:, None] * mix
        h = rmsnorm(x, norm2_g[l]) * (1.0 + sc2[:, None]) + sh2[:, None]
        x = x + g2[:, None] * hier_moe(h, w_group[l], b_group[l], w_expert[l], b_expert[l],
                                       w1[l], w3[l], w2[l])
    return rmsnorm(x, final_norm_g)
```

```python
import functools
import math

import jax
import jax.numpy as jnp
from jax import lax
from jax.experimental import pallas as pl
from jax.experimental.pallas import tpu as pltpu

F32 = jnp.float32
BF16 = jnp.bfloat16
EPS = 1e-6

ATTN_HEADS = 4
ATTN_HEAD_DIM = 64
HEAD_W = 128
REC_HEADS = 4
REC_CHUNK = 64
NUM_BUCKETS = 32
MAX_DISTANCE = 128
N_GROUPS = 4
EXPERTS_PER_GROUP = 8
N_EXPERTS = 32
TOP_K = 2
NEG = -1e30

LANES = 128
VMEM_LIMIT = 56 * 1024 * 1024

TM_PROJ = 512
T_ATTN = 512
T_REC = 512
TM_TOK = 512
BLK_E = 256


def _cparams(sem):
    return pltpu.CompilerParams(dimension_semantics=sem, vmem_limit_bytes=VMEM_LIMIT)


def _silu(x):
    return x * jax.nn.sigmoid(x)


def _mod_kernel(c_ref, w_ref, b_ref, o_ref):
    ca = _silu(c_ref[...])
    o_ref[...] = jnp.dot(ca, w_ref[...], preferred_element_type=F32,
                         precision=lax.Precision.HIGHEST) + b_ref[...]


def _mod(c, w, b):
    bsz, d = c.shape
    n = w.shape[1]
    tn = 1024
    return pl.pallas_call(
        _mod_kernel,
        grid=(n // tn,),
        in_specs=[pl.BlockSpec((bsz, d), lambda j: (0, 0)),
                  pl.BlockSpec((d, tn), lambda j: (0, j)),
                  pl.BlockSpec((1, tn), lambda j: (0, j))],
        out_specs=pl.BlockSpec((bsz, tn), lambda j: (0, j)),
        out_shape=jax.ShapeDtypeStruct((bsz, n), F32),
        compiler_params=_cparams(("arbitrary",)),
        name="mod",
    )(c, w, b.reshape(1, n))


def _norm_mod(x, g, sc, sh):
    ms = jnp.mean(x * x, axis=-1, keepdims=True)
    return (x * lax.rsqrt(ms + EPS) * g) * (1.0 + sc) + sh


def _inproj_kernel(x_ref, mod_ref, g_ref, w_ref, proj_ref, f_ref, *, f_chunk, n_chunks, cw):
    h = _norm_mod(x_ref[0], g_ref[...], mod_ref[0, 1:2, :], mod_ref[0, 0:1, :]).astype(BF16)
    for c in range(n_chunks):
        r = jnp.dot(h, w_ref[:, c * cw:(c + 1) * cw], preferred_element_type=F32)
        proj_ref[0, :, c * cw:(c + 1) * cw] = r.astype(BF16)
        if c == f_chunk:
            f_ref[0] = r


def _inproj(x, mod6, g, w_bf):
    bsz, s, d = x.shape
    n = w_bf.shape[1]
    cw = 512
    tm = min(TM_PROJ, s)
    kern = functools.partial(_inproj_kernel, f_chunk=4, n_chunks=n // cw, cw=cw)
    return pl.pallas_call(
        kern,
        grid=(bsz, s // tm),
        in_specs=[pl.BlockSpec((1, tm, d), lambda b, i: (b, i, 0)),
                  pl.BlockSpec((1, 6, d), lambda b, i: (b, 0, 0)),
                  pl.BlockSpec((1, d), lambda b, i: (0, 0)),
                  pl.BlockSpec((d, n), lambda b, i: (0, 0))],
        out_specs=[pl.BlockSpec((1, tm, n), lambda b, i: (b, i, 0)),
                   pl.BlockSpec((1, tm, cw), lambda b, i: (b, i, 0))],
        out_shape=[jax.ShapeDtypeStruct((bsz, s, n), BF16),
                   jax.ShapeDtypeStruct((bsz, s, cw), F32)],
        compiler_params=_cparams(("arbitrary", "arbitrary")),
        name="inproj",
    )(x, mod6, g.reshape(1, d), w_bf)


def _attn_kernel(lq_ref, q_ref, k_ref, v_ref, d_ref, g_ref, o_ref,
                 qs_ref, m_ref, l_ref, acc_ref, *, t, lam_init):
    qi = pl.program_id(2)
    q = q_ref[0]
    lane = lax.broadcasted_iota(jnp.int32, q.shape, 1)
    zero = jnp.zeros_like(q)
    scale = jnp.asarray(ATTN_HEAD_DIM ** -0.5, BF16)
    qs_ref[0:t, :] = jnp.where(lane < ATTN_HEAD_DIM, q, zero) * scale
    qs_ref[t:2 * t, :] = jnp.where(lane >= ATTN_HEAD_DIM, q, zero) * scale
    m_ref[...] = jnp.full(m_ref.shape, NEG, F32)
    l_ref[...] = jnp.zeros(l_ref.shape, F32)
    acc_ref[...] = jnp.zeros(acc_ref.shape, F32)

    def tile(j, bias):
        k = k_ref[0, pl.ds(pl.multiple_of(j * t, t), t), :]
        v = v_ref[0, pl.ds(pl.multiple_of(j * t, t), t), :]
        s = lax.dot_general(qs_ref[...], k, (((1,), (1,)), ((), ())), preferred_element_type=F32)
        if bias is not None:
            s = s + jnp.concatenate([bias, bias], axis=0)
        m_old = m_ref[...]
        m_new = jnp.maximum(m_old, jnp.max(s, axis=-1, keepdims=True))
        alpha = jnp.exp(m_old - m_new)
        p = jnp.exp(s - m_new)
        l_ref[...] = alpha * l_ref[...] + jnp.sum(p, axis=-1, keepdims=True)
        acc_ref[...] = alpha * acc_ref[...] + jnp.dot(p.astype(BF16), v, preferred_element_type=F32)
        m_ref[...] = m_new

    def far(j, carry):
        tile(j, None)
        return carry

    lax.fori_loop(0, qi - 1, far, 0)

    @pl.when(qi >= 1)
    def _():
        tile(qi - 1, d_ref[0, 1])

    tile(qi, d_ref[0, 0])

    lq = lq_ref[...]
    lam = (jnp.exp(jnp.sum(lq[0:1] * lq[1:2], axis=-1, keepdims=True))
           - jnp.exp(jnp.sum(lq[2:3] * lq[3:4], axis=-1, keepdims=True)) + lam_init)
    o = acc_ref[...] / l_ref[...]
    a = o[0:t] - lam * o[t:2 * t]
    ms = jnp.mean(a * a, axis=-1, keepdims=True)
    y = (a * lax.rsqrt(ms + EPS) * g_ref[...]) * (1.0 - lam_init)
    o_ref[0] = y.astype(o_ref.dtype)


def _t5_bucket(rel):
    n = jnp.maximum(rel, 0)
    max_exact = NUM_BUCKETS // 2
    nf = jnp.maximum(n, 1).astype(F32)
    large = max_exact + (jnp.log(nf / max_exact) / math.log(MAX_DISTANCE / max_exact)
                         * (NUM_BUCKETS - max_exact)).astype(jnp.int32)
    large = jnp.minimum(large, NUM_BUCKETS - 1)
    return jnp.where(n < max_exact, n, large)


def _bias_tiles(rel_table, t):
    assert t >= MAX_DISTANCE
    rel = jnp.arange(2 * t)
    vec = rel_table[_t5_bucket(rel)].astype(F32) - rel_table[NUM_BUCKETS - 1].astype(F32)[None, :]
    i = jnp.arange(t)[:, None]
    j = jnp.arange(t)[None, :]
    d0 = i - j
    diag = jnp.where((d0 >= 0)[..., None], vec[jnp.maximum(d0, 0)], NEG)
    prev = vec[d0 + t]
    return jnp.transpose(jnp.stack([diag, prev], axis=0), (3, 0, 1, 2))


def _attention(proj, lq4, subln_g, rel_table, layer):
    bsz, s, _ = proj.shape
    t = min(T_ATTN, s)
    nq = s // t
    lam_init = 0.8 - 0.6 * math.exp(-0.3 * layer)
    dt = _bias_tiles(rel_table, t)
    hq, hk, hv = 0, ATTN_HEADS, 2 * ATTN_HEADS
    kern = functools.partial(_attn_kernel, t=t, lam_init=lam_init)
    return pl.pallas_call(
        kern,
        grid=(bsz, ATTN_HEADS, nq),
        in_specs=[pl.BlockSpec((4, ATTN_HEAD_DIM), lambda b, h, i: (0, 0)),
                  pl.BlockSpec((1, t, HEAD_W), lambda b, h, i: (b, i, hq + h)),
                  pl.BlockSpec((1, s, HEAD_W), lambda b, h, i: (b, 0, hk + h)),
                  pl.BlockSpec((1, s, HEAD_W), lambda b, h, i: (b, 0, hv + h)),
                  pl.BlockSpec((1, 2, t, t), lambda b, h, i: (h, 0, 0, 0)),
                  pl.BlockSpec((1, HEAD_W), lambda b, h, i: (0, 0))],
        out_specs=pl.BlockSpec((1, t, HEAD_W), lambda b, h, i: (b, i, h)),
        out_shape=jax.ShapeDtypeStruct((bsz, s, ATTN_HEADS * HEAD_W), BF16),
        scratch_shapes=[pltpu.VMEM((2 * t, HEAD_W), BF16),
                        pltpu.VMEM((2 * t, 1), F32),
                        pltpu.VMEM((2 * t, 1), F32),
                        pltpu.VMEM((2 * t, HEAD_W), F32)],
        compiler_params=_cparams(("arbitrary", "arbitrary", "arbitrary")),
        name="attn",
    )(lq4, proj, proj, proj, dt, subln_g.reshape(1, HEAD_W))


def _hgrn_kernel(lb_ref, q_ref, f_ref, i_ref, g_ref, ng_ref, o_ref, st_ref, *, tr, layer):
    c = REC_CHUNK
    nch = tr // c

    @pl.when(pl.program_id(2) == 0)
    def _():
        st_ref[...] = jnp.zeros(st_ref.shape, F32)

    lbr = lb_ref[...]
    e = jnp.exp(lbr - jnp.max(lbr, axis=0, keepdims=True))
    lb = jnp.sum(e[0:layer + 1], axis=0, keepdims=True) / jnp.sum(e, axis=0, keepdims=True)

    fr = f_ref[0]
    sg = jax.nn.sigmoid(fr)
    lf = jnp.log(lb + (1.0 - lb) * sg)
    kf = (1.0 - lb) * jax.nn.sigmoid(-fr)
    qf = _silu(q_ref[0].astype(F32))
    vb = i_ref[0]

    row = lax.broadcasted_iota(jnp.int32, (tr, LANES), 0)
    rin = row % c

    b = lf
    sh = 1
    while sh < c:
        b = b + jnp.where(rin >= sh, pltpu.roll(b, sh, axis=0), 0.0)
        sh *= 2

    r8 = row % 8
    diags = []
    for dlt in range(8):
        if dlt == 0:
            term = qf * kf
        else:
            kd = pltpu.roll(kf, dlt, axis=0)
            bd = pltpu.roll(b, dlt, axis=0)
            term = jnp.where(r8 >= dlt, qf * kd * jnp.exp(jnp.minimum(b - bd, 0.0)), 0.0)
        diags.append(jnp.sum(term, axis=-1, keepdims=True))
    lv_q, lv_k = [], []
    for w in (8, 16, 32):
        bend = jnp.broadcast_to(b.reshape(tr // w, w, LANES)[:, w - 1:w, :],
                                (tr // w, w, LANES)).reshape(tr, LANES)
        right = (row // w) % 2 == 1
        pend = pltpu.roll(bend, w, axis=0)
        lv_q.append(jnp.where(right, qf * jnp.exp(jnp.minimum(b - pend, 0.0)), 0.0).astype(BF16))
        lv_k.append(jnp.where(right, 0.0, kf * jnp.exp(jnp.minimum(bend - b, 0.0))).astype(BF16))

    blast = jnp.broadcast_to(b.reshape(nch, c, LANES)[:, c - 1:c, :], (nch, c, LANES)).reshape(tr, LANES)
    qdec = (qf * jnp.exp(b)).astype(BF16)
    kdec = (kf * jnp.exp(blast - b)).astype(BF16)
    dec = jnp.exp(blast)

    ri = lax.broadcasted_iota(jnp.int32, (c, c), 0)
    ci = lax.broadcasted_iota(jnp.int32, (c, c), 1)
    nt = (((1,), (1,)), ((), ()))
    outs = []
    for ch in range(nch):
        sl = slice(ch * c, (ch + 1) * c)
        a = jnp.zeros((c, c), F32)
        for li, w in enumerate((8, 16, 32)):
            sw = lax.dot_general(lv_q[li][sl], lv_k[li][sl], nt, preferred_element_type=F32)
            a = a + jnp.where(ri // (2 * w) == ci // (2 * w), sw, 0.0)
        for dlt in range(8):
            a = a + jnp.where(ci == ri - dlt, diags[dlt][sl], 0.0)
        st = st_ref[...]
        o = lax.dot_general(qdec[sl], st.astype(BF16), nt, preferred_element_type=F32)
        o = o + jnp.dot(a.astype(BF16), vb[sl], preferred_element_type=F32)
        outs.append(o)
        upd = lax.dot_general(vb[sl], kdec[sl], (((0,), (0,)), ((), ())), preferred_element_type=F32)
        st_ref[...] = dec[ch * c:ch * c + 1, :] * st + upd

    o = jnp.concatenate(outs, axis=0)
    ms = jnp.mean(o * o, axis=-1, keepdims=True)
    y = (o * lax.rsqrt(ms + EPS) * ng_ref[...]) * _silu(g_ref[0].astype(F32))
    o_ref[0] = y.astype(o_ref.dtype)


def _hgrn(proj, f32f, lb_raw, norm_g, layer):
    bsz, s, _ = proj.shape
    tr = min(T_REC, s)
    nl = lb_raw.shape[0]
    cq, ci, cg = 12, 20, 24
    kern = functools.partial(_hgrn_kernel, tr=tr, layer=layer)
    return pl.pallas_call(
        kern,
        grid=(bsz, REC_HEADS, s // tr),
        in_specs=[pl.BlockSpec((nl, HEAD_W), lambda b, h, i: (0, h)),
                  pl.BlockSpec((1, tr, HEAD_W), lambda b, h, i: (b, i, cq + h)),
                  pl.BlockSpec((1, tr, HEAD_W), lambda b, h, i: (b, i, h)),
                  pl.BlockSpec((1, tr, HEAD_W), lambda b, h, i: (b, i, ci + h)),
                  pl.BlockSpec((1, tr, HEAD_W), lambda b, h, i: (b, i, cg + h)),
                  pl.BlockSpec((1, HEAD_W), lambda b, h, i: (0, h))],
        out_specs=pl.BlockSpec((1, tr, HEAD_W), lambda b, h, i: (b, i, h)),
        out_shape=jax.ShapeDtypeStruct((bsz, s, REC_HEADS * HEAD_W), BF16),
        scratch_shapes=[pltpu.VMEM((HEAD_W, HEAD_W), F32)],
        compiler_params=_cparams(("arbitrary", "arbitrary", "arbitrary")),
        name="hgrn",
    )(lb_raw, proj, f32f, proj, proj, norm_g.reshape(1, -1))


def _pack_bf16_pairs(x):
    n = x.shape[1] // 2
    lo = pltpu.bitcast(x[:, :n].astype(BF16).astype(F32), jnp.uint32)
    hi = pltpu.bitcast(x[:, n:].astype(BF16).astype(F32), jnp.uint32)
    return (hi & jnp.uint32(0xFFFF0000)) | (lo >> 16)


def _unpack_bf16_pairs(w):
    lo = pltpu.bitcast(w << 16, F32)
    hi = pltpu.bitcast(w & jnp.uint32(0xFFFF0000), F32)
    return lo, hi


def _outproj_kernel(ya_ref, yr_ref, x_ref, mod_ref, g_ref, wo_ref, wr_ref, br_ref,
                    x1_ref, h2_ref, eid_ref, wgt_ref, rank_ref, cnt_ref, run_ref, *, tm):
    first = (pl.program_id(0) == 0) & (pl.program_id(1) == 0)

    @pl.when(first)
    def _():
        run_ref[...] = jnp.zeros(run_ref.shape, F32)

    half = ya_ref.shape[2]
    mix = (jnp.dot(ya_ref[0], wo_ref[0:half, :], preferred_element_type=F32)
           + jnp.dot(yr_ref[0], wo_ref[half:2 * half, :], preferred_element_type=F32))
    x1 = x_ref[0] + mod_ref[0, 2:3, :] * mix
    x1_ref[0] = x1
    h2 = _norm_mod(x1, g_ref[...], mod_ref[0, 4:5, :], mod_ref[0, 3:4, :])
    h2_ref[0] = _pack_bf16_pairs(h2)

    logits = jnp.dot(h2.astype(BF16), wr_ref[...], preferred_element_type=F32) + br_ref[...]
    lane = lax.broadcasted_iota(jnp.int32, logits.shape, 1)
    lane_f = lane.astype(F32)

    def first_lane(mask):
        return jnp.min(jnp.where(mask, lane_f, float(LANES)), axis=-1, keepdims=True).astype(jnp.int32)

    is_g = lane < N_GROUPS
    gl = jnp.where(is_g, logits, NEG)
    gmax = jnp.max(gl, axis=-1, keepdims=True)
    gsum = jnp.sum(jnp.exp(gl - gmax), axis=-1, keepdims=True)
    g_gate = 1.0 / gsum
    g_idx = first_lane(is_g & (gl == gmax))
    e_lo = N_GROUPS + g_idx * EXPERTS_PER_GROUP
    in_grp = (lane >= e_lo) & (lane < e_lo + EXPERTS_PER_GROUP)
    el = jnp.where(in_grp, logits, NEG)
    m1 = jnp.max(el, axis=-1, keepdims=True)
    i1 = first_lane(in_grp & (el == m1))
    el2 = jnp.where(lane == i1, NEG, el)
    m2 = jnp.max(el2, axis=-1, keepdims=True)
    i2 = first_lane(in_grp & (el2 == m2))
    r = jnp.exp(m2 - m1)
    w1 = g_gate / (1.0 + r)
    w2 = g_gate * r / (1.0 + r)
    e1 = i1 - N_GROUPS
    e2 = i2 - N_GROUPS
    lane2 = lax.broadcasted_iota(jnp.int32, (tm, 2), 1)
    eid_ref[0] = jnp.where(lane2 == 0, e1, e2)
    wgt_ref[0] = jnp.where(lane2 == 0, w1, w2)

    oh1 = (lane == e1).astype(F32)
    oh2 = (lane == e2).astype(F32)
    oh = oh1 + oh2
    ri = lax.broadcasted_iota(jnp.int32, (tm, tm), 0)
    ci = lax.broadcasted_iota(jnp.int32, (tm, tm), 1)
    tri = (ci < ri).astype(BF16)
    pre = jnp.dot(tri, oh.astype(BF16), preferred_element_type=F32) + run_ref[0:1, :]
    rk1 = jnp.sum(pre * oh1, axis=-1, keepdims=True)
    rk2 = jnp.sum(pre * oh2, axis=-1, keepdims=True)
    rank_ref[0] = jnp.where(lane2 == 0, rk1, rk2).astype(jnp.int32)
    run_ref[0:1, :] = run_ref[0:1, :] + jnp.sum(oh, axis=0, keepdims=True)
    cnt_ref[...] = run_ref[...]


def _outproj(ya, yr, x, mod6, g2n, wo_bf, wr_bf, br):
    bsz, s, d = x.shape
    tm = min(TM_PROJ, s)
    half = ya.shape[2]
    kern = functools.partial(_outproj_kernel, tm=tm)
    tok = lambda b, i: (b, i, 0)
    return pl.pallas_call(
        kern,
        grid=(bsz, s // tm),
        in_specs=[pl.BlockSpec((1, tm, half), tok),
                  pl.BlockSpec((1, tm, half), tok),
                  pl.BlockSpec((1, tm, d), tok),
                  pl.BlockSpec((1, 6, d), lambda b, i: (b, 0, 0)),
                  pl.BlockSpec((1, d), lambda b, i: (0, 0)),
                  pl.BlockSpec((2 * half, d), lambda b, i: (0, 0)),
                  pl.BlockSpec((d, LANES), lambda b, i: (0, 0)),
                  pl.BlockSpec((1, LANES), lambda b, i: (0, 0))],
        out_specs=[pl.BlockSpec((1, tm, d), tok),
                   pl.BlockSpec((1, tm, d // 2), tok),
                   pl.BlockSpec((1, tm, 2), tok),
                   pl.BlockSpec((1, tm, 2), tok),
                   pl.BlockSpec((1, tm, 2), tok),
                   pl.BlockSpec((8, LANES), lambda b, i: (0, 0))],
        out_shape=[jax.ShapeDtypeStruct((bsz, s, d), F32),
                   jax.ShapeDtypeStruct((bsz, s, d // 2), jnp.uint32),
                   jax.ShapeDtypeStruct((bsz, s, 2), jnp.int32),
                   jax.ShapeDtypeStruct((bsz, s, 2), F32),
                   jax.ShapeDtypeStruct((bsz, s, 2), jnp.int32),
                   jax.ShapeDtypeStruct((8, LANES), F32)],
        scratch_shapes=[pltpu.VMEM((8, LANES), F32)],
        compiler_params=_cparams(("arbitrary", "arbitrary")),
        name="outproj",
    )(ya, yr, x, mod6, g2n.reshape(1, d), wo_bf, wr_bf, br)


def _dispatch_kernel(dest_ref, h_ref, rows_ref, sem, *, tm):
    def row_copy(n, kk):
        d = dest_ref[n * TOP_K + kk]
        return pltpu.make_async_copy(h_ref.at[pl.ds(n, 1)], rows_ref.at[pl.ds(d, 1)], sem)

    def start(n, carry):
        for kk in range(TOP_K):
            row_copy(n, kk).start()
        return carry

    def wait(n, carry):
        for kk in range(TOP_K):
            row_copy(n, kk).wait()
        return carry

    lax.fori_loop(0, tm, start, 0)
    lax.fori_loop(0, tm, wait, 0)


def _dispatch(dest, h2p):
    n, dw = h2p.shape
    tm = min(TM_TOK, n)
    kern = functools.partial(_dispatch_kernel, tm=tm)
    return pl.pallas_call(
        kern,
        grid=(n // tm,),
        in_specs=[pl.BlockSpec((tm * TOP_K,), lambda i: (i,), memory_space=pltpu.SMEM),
                  pl.BlockSpec((tm, dw), lambda i: (i, 0))],
        out_specs=pl.BlockSpec(memory_space=pl.ANY),
        scratch_shapes=[pltpu.SemaphoreType.DMA(())],
        out_shape=jax.ShapeDtypeStruct((n * TOP_K, dw), jnp.uint32),
        compiler_params=_cparams(("arbitrary",)),
        name="dispatch",
    )(dest, h2p)


def _expert_kernel(wblk_ref, wexp_ref, wlo_ref, whi_ref, rows_ref, w1_ref, w3_ref, w2_ref, y_ref,
                   w1b_ref, w3b_ref, w2b_ref, acc_ref, *, blk, n_work):
    w = pl.program_id(0)
    prev = jnp.maximum(w - 1, 0)
    nxt = jnp.minimum(w + 1, n_work - 1)
    new_exp = (w == 0) | (wexp_ref[w] != wexp_ref[prev])
    new_blk = (w == 0) | (wblk_ref[w] != wblk_ref[prev])
    last_of_blk = (w == n_work - 1) | (wblk_ref[w] != wblk_ref[nxt])
    lo = wlo_ref[w]
    hi = whi_ref[w]

    @pl.when(new_exp)
    def _():
        w1b_ref[...] = w1_ref[0].astype(BF16)
        w3b_ref[...] = w3_ref[0].astype(BF16)
        w2b_ref[...] = w2_ref[0].astype(BF16)

    @pl.when(new_blk)
    def _():
        acc_ref[...] = jnp.zeros(acc_ref.shape, F32)

    @pl.when(hi > lo)
    def _():
        xlo, xhi = _unpack_bf16_pairs(rows_ref[...])
        half = xlo.shape[1]
        xlo = xlo.astype(BF16)
        xhi = xhi.astype(BF16)
        a = (jnp.dot(xlo, w1b_ref[0:half, :], preferred_element_type=F32)
             + jnp.dot(xhi, w1b_ref[half:2 * half, :], preferred_element_type=F32))
        b = (jnp.dot(xlo, w3b_ref[0:half, :], preferred_element_type=F32)
             + jnp.dot(xhi, w3b_ref[half:2 * half, :], preferred_element_type=F32))
        hmid = (_silu(a) * b).astype(BF16)
        y = jnp.dot(hmid, w2b_ref[...], preferred_element_type=F32)
        row = lax.broadcasted_iota(jnp.int32, (blk, 1), 0)
        keep = (row >= lo) & (row < hi)
        acc_ref[...] = acc_ref[...] + jnp.where(keep, y, 0.0)

    @pl.when(last_of_blk)
    def _():
        y_ref[...] = _pack_bf16_pairs(acc_ref[...])


def _experts(wblk, wexp, wlo, whi, rows, w1, w3, w2):
    a_rows, dw = rows.shape
    n_work = wblk.shape[0]
    blk = min(BLK_E, a_rows)
    _, d, ff = w1.shape
    kern = functools.partial(_expert_kernel, blk=blk, n_work=n_work)
    return pl.pallas_call(
        kern,
        grid_spec=pltpu.PrefetchScalarGridSpec(
            num_scalar_prefetch=4,
            grid=(n_work,),
            in_specs=[pl.BlockSpec((blk, dw), lambda w, wb, we, wl, wh: (wb[w], 0)),
                      pl.BlockSpec((1, d, ff), lambda w, wb, we, wl, wh: (we[w], 0, 0)),
                      pl.BlockSpec((1, d, ff), lambda w, wb, we, wl, wh: (we[w], 0, 0)),
                      pl.BlockSpec((1, ff, d), lambda w, wb, we, wl, wh: (we[w], 0, 0))],
            out_specs=pl.BlockSpec((blk, dw), lambda w, wb, we, wl, wh: (wb[w], 0)),
            scratch_shapes=[pltpu.VMEM((d, ff), BF16),
                            pltpu.VMEM((d, ff), BF16),
                            pltpu.VMEM((ff, d), BF16),
                            pltpu.VMEM((blk, d), F32)]),
        out_shape=jax.ShapeDtypeStruct((a_rows, dw), jnp.uint32),
        compiler_params=_cparams(("arbitrary",)),
        name="experts",
    )(wblk, wexp, wlo, whi, rows, w1, w3, w2)


def _work_items(counts, a_rows, blk):
    n_blk = a_rows // blk
    ends = jnp.cumsum(counts)
    starts = ends - counts
    pts = jnp.sort(jnp.concatenate([jnp.arange(n_blk, dtype=jnp.int32) * blk, starts[1:]]))
    nxt = jnp.concatenate([pts[1:], jnp.array([a_rows], jnp.int32)])
    wblk = jnp.minimum(pts // blk, n_blk - 1)
    wexp = jnp.clip(jnp.searchsorted(ends, pts, side="right"), 0, N_EXPERTS - 1).astype(jnp.int32)
    lo = pts - wblk * blk
    hi = jnp.minimum(nxt, (wblk + 1) * blk) - wblk * blk
    hi = jnp.maximum(hi, lo)
    return wblk.astype(jnp.int32), wexp, lo.astype(jnp.int32), hi.astype(jnp.int32)


def _combine_kernel(dest_ref, y_ref, x1_ref, wgt_ref, mod_ref, g_ref, o_ref, buf_ref, sem, *, tm):
    def row_copy(n, kk):
        d = dest_ref[n * TOP_K + kk]
        return pltpu.make_async_copy(y_ref.at[pl.ds(d, 1)], buf_ref.at[kk, pl.ds(n, 1)], sem)

    def start(n, carry):
        for kk in range(TOP_K):
            row_copy(n, kk).start()
        return carry

    def wait(n, carry):
        for kk in range(TOP_K):
            row_copy(n, kk).wait()
        return carry

    lax.fori_loop(0, tm, start, 0)
    lax.fori_loop(0, tm, wait, 0)

    wg = wgt_ref[0]
    lo0, hi0 = _unpack_bf16_pairs(buf_ref[0])
    lo1, hi1 = _unpack_bf16_pairs(buf_ref[1])
    w0 = wg[:, 0:1]
    w1 = wg[:, 1:2]
    moe = jnp.concatenate([w0 * lo0 + w1 * lo1, w0 * hi0 + w1 * hi1], axis=1)
    x2 = x1_ref[0] + mod_ref[0, 5:6, :] * moe
    ms = jnp.mean(x2 * x2, axis=-1, keepdims=True)
    o_ref[0] = x2 * lax.rsqrt(ms + EPS) * g_ref[...]


def _combine(dest, ybuf, x1, wgt, mod6, gf):
    bsz, s, d = x1.shape
    tm = min(TM_TOK, s)
    kern = functools.partial(_combine_kernel, tm=tm)
    tok = lambda b, i: (b, i, 0)
    nt = s // tm
    return pl.pallas_call(
        kern,
        grid=(bsz, nt),
        in_specs=[pl.BlockSpec((tm * TOP_K,), lambda b, i: (b * nt + i,), memory_space=pltpu.SMEM),
                  pl.BlockSpec(memory_space=pl.ANY),
                  pl.BlockSpec((1, tm, d), tok),
                  pl.BlockSpec((1, tm, 2), tok),
                  pl.BlockSpec((1, 6, d), lambda b, i: (b, 0, 0)),
                  pl.BlockSpec((1, d), lambda b, i: (0, 0))],
        out_specs=pl.BlockSpec((1, tm, d), tok),
        scratch_shapes=[pltpu.VMEM((TOP_K, tm, d // 2), jnp.uint32),
                        pltpu.SemaphoreType.DMA(())],
        out_shape=jax.ShapeDtypeStruct((bsz, s, d), F32),
        compiler_params=_cparams(("arbitrary", "arbitrary")),
        name="combine",
    )(dest, ybuf, x1, wgt, mod6, gf.reshape(1, d))


def kernel(x, c, w_ada, b_ada, norm1_g, norm2_g, w_in, attn_lambda_q1, attn_lambda_k1, attn_lambda_q2,
           attn_lambda_k2, attn_subln_g, rel_bias_table, rec_lower_bound, rec_norm_g, w_out, w_group,
           b_group, w_expert, b_expert, w1, w3, w2, final_norm_g):
    bsz, s, d = x.shape
    depth = w_ada.shape[0]
    n_tok = bsz * s
    for l in range(depth):
        mod6 = _mod(c, w_ada[l], b_ada[l]).reshape(bsz, 6, d)
        proj, f32f = _inproj(x, mod6, norm1_g[l], w_in[l].astype(BF16))
        lq4 = jnp.stack([attn_lambda_q1[l], attn_lambda_k1[l], attn_lambda_q2[l], attn_lambda_k2[l]])
        ya = _attention(proj, lq4, attn_subln_g[l], rel_bias_table, l)
        yr = _hgrn(proj, f32f, rec_lower_bound, rec_norm_g[l], l)
        w_r = jnp.zeros((d, LANES), F32).at[:, :N_GROUPS].set(w_group[l])
        w_r = w_r.at[:, N_GROUPS:N_GROUPS + N_EXPERTS].set(w_expert[l]).astype(BF16)
        b_r = jnp.zeros((1, LANES), F32).at[0, :N_GROUPS].set(b_group[l])
        b_r = b_r.at[0, N_GROUPS:N_GROUPS + N_EXPERTS].set(b_expert[l])
        x1, h2p, eid, wgt, rank, cnt = _outproj(ya, yr, x, mod6, norm2_g[l], w_out[l].astype(BF16), w_r, b_r)
        counts = cnt[0, :N_EXPERTS].astype(jnp.int32)
        starts = jnp.cumsum(counts) - counts
        dest = (starts[eid] + rank).reshape(n_tok * TOP_K)
        rows = _dispatch(dest, h2p.reshape(n_tok, d // 2))
        wblk, wexp, wlo, whi = _work_items(counts, n_tok * TOP_K, min(BLK_E, n_tok * TOP_K))
        ybuf = _experts(wblk, wexp, wlo, whi, rows, w1[l], w3[l], w2[l])
        gf = final_norm_g if l == depth - 1 else jnp.ones((d,), F32)
        assert depth == 1
        x = _combine(dest, ybuf, x1, wgt, mod6, gf)
    return x
```

```python
import functools
import math

import jax
import jax.numpy as jnp
from jax import lax
from jax.experimental import pallas as pl
from jax.experimental.pallas import tpu as pltpu

F32 = jnp.float32
BF16 = jnp.bfloat16
EPS = 1e-6

ATTN_HEADS = 4
ATTN_HEAD_DIM = 64
HEAD_W = 128
REC_HEADS = 4
REC_CHUNK = 64
NUM_BUCKETS = 32
MAX_DISTANCE = 128
N_GROUPS = 4
EXPERTS_PER_GROUP = 8
N_EXPERTS = 32
TOP_K = 2
NEG = -1e30

LANES = 128
VMEM_LIMIT = 56 * 1024 * 1024

TM_PROJ = 512
T_ATTN = 512
T_REC = 512
TM_TOK = 512
BLK_E = 256


def _cparams(sem):
    return pltpu.CompilerParams(dimension_semantics=sem, vmem_limit_bytes=VMEM_LIMIT)


def _silu(x):
    return x * jax.nn.sigmoid(x)


def _mod_kernel(c_ref, w_ref, b_ref, o_ref):
    ca = _silu(c_ref[...])
    o_ref[...] = jnp.dot(ca, w_ref[...], preferred_element_type=F32,
                         precision=lax.Precision.HIGHEST) + b_ref[...]


def _mod(c, w, b):
    bsz, d = c.shape
    n = w.shape[1]
    tn = 1024
    return pl.pallas_call(
        _mod_kernel,
        grid=(n // tn,),
        in_specs=[pl.BlockSpec((bsz, d), lambda j: (0, 0)),
                  pl.BlockSpec((d, tn), lambda j: (0, j)),
                  pl.BlockSpec((1, tn), lambda j: (0, j))],
        out_specs=pl.BlockSpec((bsz, tn), lambda j: (0, j)),
        out_shape=jax.ShapeDtypeStruct((bsz, n), F32),
        compiler_params=_cparams(("arbitrary",)),
        name="mod",
    )(c, w, b.reshape(1, n))


def _norm_mod(x, g, sc, sh):
    ms = jnp.mean(x * x, axis=-1, keepdims=True)
    return (x * lax.rsqrt(ms + EPS) * g) * (1.0 + sc) + sh


def _inproj_kernel(x_ref, mod_ref, g_ref, w_ref, proj_ref, f_ref, *, f_chunk, n_chunks, cw):
    h = _norm_mod(x_ref[0], g_ref[...], mod_ref[0, 1:2, :], mod_ref[0, 0:1, :]).astype(BF16)
    for c in range(n_chunks):
        r = jnp.dot(h, w_ref[:, c * cw:(c + 1) * cw], preferred_element_type=F32)
        proj_ref[0, :, c * cw:(c + 1) * cw] = r.astype(BF16)
        if c == f_chunk:
            f_ref[0] = r


def _inproj(x, mod6, g, w_bf):
    bsz, s, d = x.shape
    n = w_bf.shape[1]
    cw = 512
    tm = min(TM_PROJ, s)
    kern = functools.partial(_inproj_kernel, f_chunk=4, n_chunks=n // cw, cw=cw)
    return pl.pallas_call(
        kern,
        grid=(bsz, s // tm),
        in_specs=[pl.BlockSpec((1, tm, d), lambda b, i: (b, i, 0)),
                  pl.BlockSpec((1, 6, d), lambda b, i: (b, 0, 0)),
                  pl.BlockSpec((1, d), lambda b, i: (0, 0)),
                  pl.BlockSpec((d, n), lambda b, i: (0, 0))],
        out_specs=[pl.BlockSpec((1, tm, n), lambda b, i: (b, i, 0)),
                   pl.BlockSpec((1, tm, cw), lambda b, i: (b, i, 0))],
        out_shape=[jax.ShapeDtypeStruct((bsz, s, n), BF16),
                   jax.ShapeDtypeStruct((bsz, s, cw), F32)],
        compiler_params=_cparams(("arbitrary", "arbitrary")),
        name="inproj",
    )(x, mod6, g.reshape(1, d), w_bf)


LOG2E = math.log2(math.e)


def _attn_kernel(lq_ref, q_ref, k_ref, v_ref, d_ref, g_ref, o_ref,
                 qs_ref, vt_ref, m_ref, l_ref, acc_ref, *, t, lam_init):
    qi = pl.program_id(2)
    nkt = vt_ref.shape[0]

    @pl.when(qi == 0)
    def _():
        for c in range(nkt):
            vt_ref[c] = v_ref[0, c * t:(c + 1) * t, :].T

    q = q_ref[0]
    lane = lax.broadcasted_iota(jnp.int32, q.shape, 1)
    qf = q.astype(F32) * (ATTN_HEAD_DIM ** -0.5 * LOG2E)
    zero = jnp.zeros_like(qf)
    qs_ref[0:t, :] = jnp.where(lane < ATTN_HEAD_DIM, qf, zero).astype(BF16)
    qs_ref[t:2 * t, :] = jnp.where(lane >= ATTN_HEAD_DIM, qf, zero).astype(BF16)
    m_ref[...] = jnp.full(m_ref.shape, NEG, F32)
    l_ref[...] = jnp.zeros(l_ref.shape, F32)
    acc_ref[...] = jnp.zeros(acc_ref.shape, F32)

    def tile(j, bias):
        k = k_ref[0, pl.ds(pl.multiple_of(j * t, t), t), :]
        s = lax.dot_general(k, qs_ref[...], (((1,), (1,)), ((), ())), preferred_element_type=F32)
        if bias is not None:
            s = s + bias
        m_old = m_ref[...]
        m_new = jnp.maximum(m_old, jnp.max(s, axis=0, keepdims=True))
        alpha = jnp.exp2(m_old - m_new)
        p = jnp.exp2(s - m_new)
        l_ref[...] = alpha * l_ref[...] + jnp.sum(p, axis=0, keepdims=True)
        acc_ref[...] = alpha * acc_ref[...] + jnp.dot(vt_ref[j], p.astype(BF16), preferred_element_type=F32)
        m_ref[...] = m_new

    def far(j, carry):
        tile(j, None)
        return carry

    lax.fori_loop(0, qi - 1, far, 0)

    @pl.when(qi >= 1)
    def _():
        tile(qi - 1, d_ref[0, 1])

    tile(qi, d_ref[0, 0])

    lq = lq_ref[...]
    lam = (jnp.exp(jnp.sum(lq[0:1] * lq[1:2], axis=-1, keepdims=True))
           - jnp.exp(jnp.sum(lq[2:3] * lq[3:4], axis=-1, keepdims=True)) + lam_init)
    o = acc_ref[...] / l_ref[...]
    a = o[:, 0:t] - lam * o[:, t:2 * t]
    ms = jnp.mean(a * a, axis=0, keepdims=True)
    y = (a * lax.rsqrt(ms + EPS)).T * (g_ref[...] * (1.0 - lam_init))
    o_ref[0] = y.astype(o_ref.dtype)


def _t5_bucket(rel):
    n = jnp.maximum(rel, 0)
    max_exact = NUM_BUCKETS // 2
    nf = jnp.maximum(n, 1).astype(F32)
    large = max_exact + (jnp.log(nf / max_exact) / math.log(MAX_DISTANCE / max_exact)
                         * (NUM_BUCKETS - max_exact)).astype(jnp.int32)
    large = jnp.minimum(large, NUM_BUCKETS - 1)
    return jnp.where(n < max_exact, n, large)


def _bias_tiles(rel_table, t):
    assert t >= MAX_DISTANCE
    nh = rel_table.shape[1]
    rel = jnp.arange(2 * t)
    vec = (rel_table[_t5_bucket(rel)].astype(F32) - rel_table[NUM_BUCKETS - 1].astype(F32)[None, :]).T * LOG2E
    neg = jnp.full((nh, t), NEG, F32)

    def toeplitz(w):
        flat = jnp.tile(w, (1, t))[:, :t * (2 * t - 1)]
        return flat.reshape(nh, t, 2 * t - 1)[:, :, :t]

    diag = toeplitz(jnp.concatenate([vec[:, :t], neg], axis=1))
    prev = toeplitz(jnp.concatenate([vec[:, t:], vec[:, :t]], axis=1))
    both = jnp.stack([diag, prev], axis=1)
    return jnp.concatenate([both, both], axis=-1)


def _attention(proj, lq4, subln_g, rel_table, layer):
    bsz, s, _ = proj.shape
    t = min(T_ATTN, s)
    nq = s // t
    lam_init = 0.8 - 0.6 * math.exp(-0.3 * layer)
    dt = _bias_tiles(rel_table, t)
    hq, hk, hv = 0, ATTN_HEADS, 2 * ATTN_HEADS
    kern = functools.partial(_attn_kernel, t=t, lam_init=lam_init)
    return pl.pallas_call(
        kern,
        grid=(bsz, ATTN_HEADS, nq),
        in_specs=[pl.BlockSpec((4, ATTN_HEAD_DIM), lambda b, h, i: (0, 0)),
                  pl.BlockSpec((1, t, HEAD_W), lambda b, h, i: (b, i, hq + h)),
                  pl.BlockSpec((1, s, HEAD_W), lambda b, h, i: (b, 0, hk + h)),
                  pl.BlockSpec((1, s, HEAD_W), lambda b, h, i: (b, 0, hv + h)),
                  pl.BlockSpec((1, 2, t, 2 * t), lambda b, h, i: (h, 0, 0, 0)),
                  pl.BlockSpec((1, HEAD_W), lambda b, h, i: (0, 0))],
        out_specs=pl.BlockSpec((1, t, HEAD_W), lambda b, h, i: (b, i, h)),
        out_shape=jax.ShapeDtypeStruct((bsz, s, ATTN_HEADS * HEAD_W), BF16),
        scratch_shapes=[pltpu.VMEM((2 * t, HEAD_W), BF16),
                        pltpu.VMEM((nq, HEAD_W, t), BF16),
                        pltpu.VMEM((1, 2 * t), F32),
                        pltpu.VMEM((1, 2 * t), F32),
                        pltpu.VMEM((HEAD_W, 2 * t), F32)],
        compiler_params=_cparams(("arbitrary", "arbitrary", "arbitrary")),
        name="attn",
    )(lq4, proj, proj, proj, dt, subln_g.reshape(1, HEAD_W))


def _hgrn_kernel(lb_ref, q_ref, f_ref, i_ref, g_ref, ng_ref, o_ref, st_ref, *, tr, layer):
    c = REC_CHUNK
    nch = tr // c

    @pl.when(pl.program_id(2) == 0)
    def _():
        st_ref[...] = jnp.zeros(st_ref.shape, F32)

    lbr = lb_ref[...]
    e = jnp.exp(lbr - jnp.max(lbr, axis=0, keepdims=True))
    lb = jnp.sum(e[0:layer + 1], axis=0, keepdims=True) / jnp.sum(e, axis=0, keepdims=True)

    fr = f_ref[0]
    sg = jax.nn.sigmoid(fr)
    lf = jnp.log(lb + (1.0 - lb) * sg)
    kf = (1.0 - lb) * jax.nn.sigmoid(-fr)
    qf = _silu(q_ref[0].astype(F32))
    vb = i_ref[0]

    row = lax.broadcasted_iota(jnp.int32, (tr, LANES), 0)
    rin = row % c

    b = lf
    sh = 1
    while sh < c:
        b = b + jnp.where(rin >= sh, pltpu.roll(b, sh, axis=0), 0.0)
        sh *= 2

    r8 = row % 8
    diags = []
    for dlt in range(8):
        if dlt == 0:
            term = qf * kf
        else:
            kd = pltpu.roll(kf, dlt, axis=0)
            bd = pltpu.roll(b, dlt, axis=0)
            term = jnp.where(r8 >= dlt, qf * kd * jnp.exp(jnp.minimum(b - bd, 0.0)), 0.0)
        diags.append(jnp.sum(term, axis=-1, keepdims=True))
    lv_q, lv_k = [], []
    for w in (8, 16, 32):
        bend = jnp.broadcast_to(b.reshape(tr // w, w, LANES)[:, w - 1:w, :],
                                (tr // w, w, LANES)).reshape(tr, LANES)
        right = (row // w) % 2 == 1
        pend = pltpu.roll(bend, w, axis=0)
        lv_q.append(jnp.where(right, qf * jnp.exp(jnp.minimum(b - pend, 0.0)), 0.0).astype(BF16))
        lv_k.append(jnp.where(right, 0.0, kf * jnp.exp(jnp.minimum(bend - b, 0.0))).astype(BF16))

    blast = jnp.broadcast_to(b.reshape(nch, c, LANES)[:, c - 1:c, :], (nch, c, LANES)).reshape(tr, LANES)
    qdec = (qf * jnp.exp(b)).astype(BF16)
    kdec = (kf * jnp.exp(blast - b)).astype(BF16)
    dec = jnp.exp(blast)

    ri = lax.broadcasted_iota(jnp.int32, (c, c), 0)
    ci = lax.broadcasted_iota(jnp.int32, (c, c), 1)
    nt = (((1,), (1,)), ((), ()))
    outs = []
    for ch in range(nch):
        sl = slice(ch * c, (ch + 1) * c)
        a = jnp.zeros((c, c), F32)
        for li, w in enumerate((8, 16, 32)):
            sw = lax.dot_general(lv_q[li][sl], lv_k[li][sl], nt, preferred_element_type=F32)
            a = a + jnp.where(ri // (2 * w) == ci // (2 * w), sw, 0.0)
        for dlt in range(8):
            a = a + jnp.where(ci == ri - dlt, diags[dlt][sl], 0.0)
        st = st_ref[...]
        o = lax.dot_general(qdec[sl], st.astype(BF16), nt, preferred_element_type=F32)
        o = o + jnp.dot(a.astype(BF16), vb[sl], preferred_element_type=F32)
        outs.append(o)
        upd = lax.dot_general(vb[sl], kdec[sl], (((0,), (0,)), ((), ())), preferred_element_type=F32)
        st_ref[...] = dec[ch * c:ch * c + 1, :] * st + upd

    o = jnp.concatenate(outs, axis=0)
    ms = jnp.mean(o * o, axis=-1, keepdims=True)
    y = (o * lax.rsqrt(ms + EPS) * ng_ref[...]) * _silu(g_ref[0].astype(F32))
    o_ref[0] = y.astype(o_ref.dtype)


def _hgrn(proj, f32f, lb_raw, norm_g, layer):
    bsz, s, _ = proj.shape
    tr = min(T_REC, s)
    nl = lb_raw.shape[0]
    cq, ci, cg = 12, 20, 24
    kern = functools.partial(_hgrn_kernel, tr=tr, layer=layer)
    return pl.pallas_call(
        kern,
        grid=(bsz, REC_HEADS, s // tr),
        in_specs=[pl.BlockSpec((nl, HEAD_W), lambda b, h, i: (0, h)),
                  pl.BlockSpec((1, tr, HEAD_W), lambda b, h, i: (b, i, cq + h)),
                  pl.BlockSpec((1, tr, HEAD_W), lambda b, h, i: (b, i, h)),
                  pl.BlockSpec((1, tr, HEAD_W), lambda b, h, i: (b, i, ci + h)),
                  pl.BlockSpec((1, tr, HEAD_W), lambda b, h, i: (b, i, cg + h)),
                  pl.BlockSpec((1, HEAD_W), lambda b, h, i: (0, h))],
        out_specs=pl.BlockSpec((1, tr, HEAD_W), lambda b, h, i: (b, i, h)),
        out_shape=jax.ShapeDtypeStruct((bsz, s, REC_HEADS * HEAD_W), BF16),
        scratch_shapes=[pltpu.VMEM((HEAD_W, HEAD_W), F32)],
        compiler_params=_cparams(("arbitrary", "arbitrary", "arbitrary")),
        name="hgrn",
    )(lb_raw, proj, f32f, proj, proj, norm_g.reshape(1, -1))


def _pack_bf16_pairs(x):
    n = x.shape[1] // 2
    lo = pltpu.bitcast(x[:, :n].astype(BF16).astype(F32), jnp.uint32)
    hi = pltpu.bitcast(x[:, n:].astype(BF16).astype(F32), jnp.uint32)
    return (hi & jnp.uint32(0xFFFF0000)) | (lo >> 16)


def _unpack_bf16_pairs(w):
    lo = pltpu.bitcast(w << 16, F32)
    hi = pltpu.bitcast(w & jnp.uint32(0xFFFF0000), F32)
    return lo, hi


def _outproj_kernel(ya_ref, yr_ref, x_ref, mod_ref, g_ref, wo_ref, wr_ref, br_ref,
                    x1_ref, h2_ref, eid_ref, wgt_ref, rank_ref, cnt_ref, run_ref, *, tm):
    first = (pl.program_id(0) == 0) & (pl.program_id(1) == 0)

    @pl.when(first)
    def _():
        run_ref[...] = jnp.zeros(run_ref.shape, F32)

    half = ya_ref.shape[2]
    mix = (jnp.dot(ya_ref[0], wo_ref[0:half, :], preferred_element_type=F32)
           + jnp.dot(yr_ref[0], wo_ref[half:2 * half, :], preferred_element_type=F32))
    x1 = x_ref[0] + mod_ref[0, 2:3, :] * mix
    x1_ref[0] = x1
    h2 = _norm_mod(x1, g_ref[...], mod_ref[0, 4:5, :], mod_ref[0, 3:4, :])
    h2_ref[0] = _pack_bf16_pairs(h2)

    logits = jnp.dot(h2.astype(BF16), wr_ref[...], preferred_element_type=F32) + br_ref[...]
    lane = lax.broadcasted_iota(jnp.int32, logits.shape, 1)
    lane_f = lane.astype(F32)

    def first_lane(mask):
        return jnp.min(jnp.where(mask, lane_f, float(LANES)), axis=-1, keepdims=True).astype(jnp.int32)

    is_g = lane < N_GROUPS
    gl = jnp.where(is_g, logits, NEG)
    gmax = jnp.max(gl, axis=-1, keepdims=True)
    gsum = jnp.sum(jnp.exp(gl - gmax), axis=-1, keepdims=True)
    g_gate = 1.0 / gsum
    g_idx = first_lane(is_g & (gl == gmax))
    e_lo = N_GROUPS + g_idx * EXPERTS_PER_GROUP
    in_grp = (lane >= e_lo) & (lane < e_lo + EXPERTS_PER_GROUP)
    el = jnp.where(in_grp, logits, NEG)
    m1 = jnp.max(el, axis=-1, keepdims=True)
    i1 = first_lane(in_grp & (el == m1))
    el2 = jnp.where(lane == i1, NEG, el)
    m2 = jnp.max(el2, axis=-1, keepdims=True)
    i2 = first_lane(in_grp & (el2 == m2))
    r = jnp.exp(m2 - m1)
    w1 = g_gate / (1.0 + r)
    w2 = g_gate * r / (1.0 + r)
    e1 = i1 - N_GROUPS
    e2 = i2 - N_GROUPS
    lane2 = lax.broadcasted_iota(jnp.int32, (tm, 2), 1)
    eid_ref[0] = jnp.where(lane2 == 0, e1, e2)
    wgt_ref[0] = jnp.where(lane2 == 0, w1, w2)

    oh1 = (lane == e1).astype(F32)
    oh2 = (lane == e2).astype(F32)
    oh = oh1 + oh2
    ri = lax.broadcasted_iota(jnp.int32, (tm, tm), 0)
    ci = lax.broadcasted_iota(jnp.int32, (tm, tm), 1)
    tri = (ci < ri).astype(BF16)
    pre = jnp.dot(tri, oh.astype(BF16), preferred_element_type=F32) + run_ref[0:1, :]
    rk1 = jnp.sum(pre * oh1, axis=-1, keepdims=True)
    rk2 = jnp.sum(pre * oh2, axis=-1, keepdims=True)
    rank_ref[0] = jnp.where(lane2 == 0, rk1, rk2).astype(jnp.int32)
    run_ref[0:1, :] = run_ref[0:1, :] + jnp.sum(oh, axis=0, keepdims=True)
    cnt_ref[...] = run_ref[...]


def _outproj(ya, yr, x, mod6, g2n, wo_bf, wr_bf, br):
    bsz, s, d = x.shape
    tm = min(TM_PROJ, s)
    half = ya.shape[2]
    kern = functools.partial(_outproj_kernel, tm=tm)
    tok = lambda b, i: (b, i, 0)
    return pl.pallas_call(
        kern,
        grid=(bsz, s // tm),
        in_specs=[pl.BlockSpec((1, tm, half), tok),
                  pl.BlockSpec((1, tm, half), tok),
                  pl.BlockSpec((1, tm, d), tok),
                  pl.BlockSpec((1, 6, d), lambda b, i: (b, 0, 0)),
                  pl.BlockSpec((1, d), lambda b, i: (0, 0)),
                  pl.BlockSpec((2 * half, d), lambda b, i: (0, 0)),
                  pl.BlockSpec((d, LANES), lambda b, i: (0, 0)),
                  pl.BlockSpec((1, LANES), lambda b, i: (0, 0))],
        out_specs=[pl.BlockSpec((1, tm, d), tok),
                   pl.BlockSpec((1, tm, d // 2), tok),
                   pl.BlockSpec((1, tm, 2), tok),
                   pl.BlockSpec((1, tm, 2), tok),
                   pl.BlockSpec((1, tm, 2), tok),
                   pl.BlockSpec((8, LANES), lambda b, i: (0, 0))],
        out_shape=[jax.ShapeDtypeStruct((bsz, s, d), F32),
                   jax.ShapeDtypeStruct((bsz, s, d // 2), jnp.uint32),
                   jax.ShapeDtypeStruct((bsz, s, 2), jnp.int32),
                   jax.ShapeDtypeStruct((bsz, s, 2), F32),
                   jax.ShapeDtypeStruct((bsz, s, 2), jnp.int32),
                   jax.ShapeDtypeStruct((8, LANES), F32)],
        scratch_shapes=[pltpu.VMEM((8, LANES), F32)],
        compiler_params=_cparams(("arbitrary", "arbitrary")),
        name="outproj",
    )(ya, yr, x, mod6, g2n.reshape(1, d), wo_bf, wr_bf, br)


def _dispatch_kernel(dest_ref, h_ref, rows_ref, sem, *, tm):
    def row_copy(n, kk):
        d = dest_ref[n * TOP_K + kk]
        return pltpu.make_async_copy(h_ref.at[pl.ds(n, 1)], rows_ref.at[pl.ds(d, 1)], sem)

    def start(n, carry):
        for kk in range(TOP_K):
            row_copy(n, kk).start()
        return carry

    def wait(n, carry):
        for kk in range(TOP_K):
            row_copy(n, kk).wait()
        return carry

    lax.fori_loop(0, tm, start, 0)
    lax.fori_loop(0, tm, wait, 0)


def _dispatch(dest, h2p):
    n, dw = h2p.shape
    tm = min(TM_TOK, n)
    kern = functools.partial(_dispatch_kernel, tm=tm)
    return pl.pallas_call(
        kern,
        grid=(n // tm,),
        in_specs=[pl.BlockSpec((tm * TOP_K,), lambda i: (i,), memory_space=pltpu.SMEM),
                  pl.BlockSpec((tm, dw), lambda i: (i, 0))],
        out_specs=pl.BlockSpec(memory_space=pl.ANY),
        scratch_shapes=[pltpu.SemaphoreType.DMA(())],
        out_shape=jax.ShapeDtypeStruct((n * TOP_K, dw), jnp.uint32),
        compiler_params=_cparams(("arbitrary",)),
        name="dispatch",
    )(dest, h2p)


def _expert_kernel(wblk_ref, wexp_ref, wlo_ref, whi_ref, rows_ref, w1_ref, w3_ref, w2_ref, y_ref,
                   w1b_ref, w3b_ref, w2b_ref, acc_ref, *, blk, n_work):
    w = pl.program_id(0)
    prev = jnp.maximum(w - 1, 0)
    nxt = jnp.minimum(w + 1, n_work - 1)
    new_exp = (w == 0) | (wexp_ref[w] != wexp_ref[prev])
    new_blk = (w == 0) | (wblk_ref[w] != wblk_ref[prev])
    last_of_blk = (w == n_work - 1) | (wblk_ref[w] != wblk_ref[nxt])
    lo = wlo_ref[w]
    hi = whi_ref[w]

    @pl.when(new_exp)
    def _():
        w1b_ref[...] = w1_ref[0].astype(BF16)
        w3b_ref[...] = w3_ref[0].astype(BF16)
        w2b_ref[...] = w2_ref[0].astype(BF16)

    @pl.when(new_blk)
    def _():
        acc_ref[...] = jnp.zeros(acc_ref.shape, F32)

    @pl.when(hi > lo)
    def _():
        xlo, xhi = _unpack_bf16_pairs(rows_ref[...])
        half = xlo.shape[1]
        xlo = xlo.astype(BF16)
        xhi = xhi.astype(BF16)
        a = (jnp.dot(xlo, w1b_ref[0:half, :], preferred_element_type=F32)
             + jnp.dot(xhi, w1b_ref[half:2 * half, :], preferred_element_type=F32))
        b = (jnp.dot(xlo, w3b_ref[0:half, :], preferred_element_type=F32)
             + jnp.dot(xhi, w3b_ref[half:2 * half, :], preferred_element_type=F32))
        hmid = (_silu(a) * b).astype(BF16)
        y = jnp.dot(hmid, w2b_ref[...], preferred_element_type=F32)
        row = lax.broadcasted_iota(jnp.int32, (blk, 1), 0)
        keep = (row >= lo) & (row < hi)
        acc_ref[...] = acc_ref[...] + jnp.where(keep, y, 0.0)

    @pl.when(last_of_blk)
    def _():
        y_ref[...] = _pack_bf16_pairs(acc_ref[...])


def _experts(wblk, wexp, wlo, whi, rows, w1, w3, w2):
    a_rows, dw = rows.shape
    n_work = wblk.shape[0]
    blk = min(BLK_E, a_rows)
    _, d, ff = w1.shape
    kern = functools.partial(_expert_kernel, blk=blk, n_work=n_work)
    return pl.pallas_call(
        kern,
        grid_spec=pltpu.PrefetchScalarGridSpec(
            num_scalar_prefetch=4,
            grid=(n_work,),
            in_specs=[pl.BlockSpec((blk, dw), lambda w, wb, we, wl, wh: (wb[w], 0)),
                      pl.BlockSpec((1, d, ff), lambda w, wb, we, wl, wh: (we[w], 0, 0)),
                      pl.BlockSpec((1, d, ff), lambda w, wb, we, wl, wh: (we[w], 0, 0)),
                      pl.BlockSpec((1, ff, d), lambda w, wb, we, wl, wh: (we[w], 0, 0))],
            out_specs=pl.BlockSpec((blk, dw), lambda w, wb, we, wl, wh: (wb[w], 0)),
            scratch_shapes=[pltpu.VMEM((d, ff), BF16),
                            pltpu.VMEM((d, ff), BF16),
                            pltpu.VMEM((ff, d), BF16),
                            pltpu.VMEM((blk, d), F32)]),
        out_shape=jax.ShapeDtypeStruct((a_rows, dw), jnp.uint32),
        compiler_params=_cparams(("arbitrary",)),
        name="experts",
    )(wblk, wexp, wlo, whi, rows, w1, w3, w2)


def _work_items(counts, a_rows, blk):
    n_blk = a_rows // blk
    ends = jnp.cumsum(counts)
    starts = ends - counts
    pts = jnp.sort(jnp.concatenate([jnp.arange(n_blk, dtype=jnp.int32) * blk, starts[1:]]))
    nxt = jnp.concatenate([pts[1:], jnp.array([a_rows], jnp.int32)])
    wblk = jnp.minimum(pts // blk, n_blk - 1)
    wexp = jnp.minimum(jnp.sum(ends[None, :] <= pts[:, None], axis=1), N_EXPERTS - 1).astype(jnp.int32)
    lo = pts - wblk * blk
    hi = jnp.minimum(nxt, (wblk + 1) * blk) - wblk * blk
    hi = jnp.maximum(hi, lo)
    return wblk.astype(jnp.int32), wexp, lo.astype(jnp.int32), hi.astype(jnp.int32)


def _combine_kernel(dest_ref, y_ref, x1_ref, wgt_ref, mod_ref, g_ref, o_ref, buf_ref, sem, *, tm):
    def row_copy(n, kk):
        d = dest_ref[n * TOP_K + kk]
        return pltpu.make_async_copy(y_ref.at[pl.ds(d, 1)], buf_ref.at[kk, pl.ds(n, 1)], sem)

    def start(n, carry):
        for kk in range(TOP_K):
            row_copy(n, kk).start()
        return carry

    def wait(n, carry):
        for kk in range(TOP_K):
            row_copy(n, kk).wait()
        return carry

    lax.fori_loop(0, tm, start, 0)
    lax.fori_loop(0, tm, wait, 0)

    wg = wgt_ref[0]
    lo0, hi0 = _unpack_bf16_pairs(buf_ref[0])
    lo1, hi1 = _unpack_bf16_pairs(buf_ref[1])
    w0 = wg[:, 0:1]
    w1 = wg[:, 1:2]
    moe = jnp.concatenate([w0 * lo0 + w1 * lo1, w0 * hi0 + w1 * hi1], axis=1)
    x2 = x1_ref[0] + mod_ref[0, 5:6, :] * moe
    ms = jnp.mean(x2 * x2, axis=-1, keepdims=True)
    o_ref[0] = x2 * lax.rsqrt(ms + EPS) * g_ref[...]


def _combine(dest, ybuf, x1, wgt, mod6, gf):
    bsz, s, d = x1.shape
    tm = min(TM_TOK, s)
    kern = functools.partial(_combine_kernel, tm=tm)
    tok = lambda b, i: (b, i, 0)
    nt = s // tm
    return pl.pallas_call(
        kern,
        grid=(bsz, nt),
        in_specs=[pl.BlockSpec((tm * TOP_K,), lambda b, i: (b * nt + i,), memory_space=pltpu.SMEM),
                  pl.BlockSpec(memory_space=pl.ANY),
                  pl.BlockSpec((1, tm, d), tok),
                  pl.BlockSpec((1, tm, 2), tok),
                  pl.BlockSpec((1, 6, d), lambda b, i: (b, 0, 0)),
                  pl.BlockSpec((1, d), lambda b, i: (0, 0))],
        out_specs=pl.BlockSpec((1, tm, d), tok),
        scratch_shapes=[pltpu.VMEM((TOP_K, tm, d // 2), jnp.uint32),
                        pltpu.SemaphoreType.DMA(())],
        out_shape=jax.ShapeDtypeStruct((bsz, s, d), F32),
        compiler_params=_cparams(("arbitrary", "arbitrary")),
        name="combine",
    )(dest, ybuf, x1, wgt, mod6, gf.reshape(1, d))


def kernel(x, c, w_ada, b_ada, norm1_g, norm2_g, w_in, attn_lambda_q1, attn_lambda_k1, attn_lambda_q2,
           attn_lambda_k2, attn_subln_g, rel_bias_table, rec_lower_bound, rec_norm_g, w_out, w_group,
           b_group, w_expert, b_expert, w1, w3, w2, final_norm_g):
    bsz, s, d = x.shape
    depth = w_ada.shape[0]
    n_tok = bsz * s
    for l in range(depth):
        mod6 = _mod(c, w_ada[l], b_ada[l]).reshape(bsz, 6, d)
        proj, f32f = _inproj(x, mod6, norm1_g[l], w_in[l].astype(BF16))
        lq4 = jnp.stack([attn_lambda_q1[l], attn_lambda_k1[l], attn_lambda_q2[l], attn_lambda_k2[l]])
        ya = _attention(proj, lq4, attn_subln_g[l], rel_bias_table, l)
        yr = _hgrn(proj, f32f, rec_lower_bound, rec_norm_g[l], l)
        w_r = jnp.zeros((d, LANES), F32).at[:, :N_GROUPS].set(w_group[l])
        w_r = w_r.at[:, N_GROUPS:N_GROUPS + N_EXPERTS].set(w_expert[l]).astype(BF16)
        b_r = jnp.zeros((1, LANES), F32).at[0, :N_GROUPS].set(b_group[l])
        b_r = b_r.at[0, N_GROUPS:N_GROUPS + N_EXPERTS].set(b_expert[l])
        x1, h2p, eid, wgt, rank, cnt = _outproj(ya, yr, x, mod6, norm2_g[l], w_out[l].astype(BF16), w_r, b_r)
        counts = cnt[0, :N_EXPERTS].astype(jnp.int32)
        starts = jnp.cumsum(counts) - counts
        eoh = eid[..., None] == jnp.arange(N_EXPERTS, dtype=jnp.int32)
        dest = (jnp.sum(jnp.where(eoh, starts, 0), axis=-1) + rank).reshape(n_tok * TOP_K)
        rows = _dispatch(dest, h2p.reshape(n_tok, d // 2))
        wblk, wexp, wlo, whi = _work_items(counts, n_tok * TOP_K, min(BLK_E, n_tok * TOP_K))
        ybuf = _experts(wblk, wexp, wlo, whi, rows, w1[l], w3[l], w2[l])
        gf = final_norm_g if l == depth - 1 else jnp.ones((d,), F32)
        assert depth == 1
        x = _combine(dest, ybuf, x1, wgt, mod6, gf)
    return x
```

```python
import functools
import math

import jax
import jax.numpy as jnp
from jax import lax
from jax.experimental import pallas as pl
from jax.experimental.pallas import tpu as pltpu

F32 = jnp.float32
BF16 = jnp.bfloat16
EPS = 1e-6

ATTN_HEADS = 4
ATTN_HEAD_DIM = 64
HEAD_W = 128
REC_HEADS = 4
REC_CHUNK = 64
NUM_BUCKETS = 32
MAX_DISTANCE = 128
N_GROUPS = 4
EXPERTS_PER_GROUP = 8
N_EXPERTS = 32
TOP_K = 2
NEG = -1e30

LANES = 128
VMEM_LIMIT = 56 * 1024 * 1024

TM_PROJ = 512
T_ATTN = 512
ATTN_COLS = 256
T_REC = 512
BLK_E = 256


def _cparams(sem):
    return pltpu.CompilerParams(dimension_semantics=sem, vmem_limit_bytes=VMEM_LIMIT)


def _silu(x):
    return x * jax.nn.sigmoid(x)


def _mod_kernel(c_ref, w_ref, b_ref, o_ref):
    ca = _silu(c_ref[...])
    o_ref[...] = jnp.dot(ca, w_ref[...], preferred_element_type=F32,
                         precision=lax.Precision.HIGHEST) + b_ref[...]


def _mod(c, w, b):
    bsz, d = c.shape
    n = w.shape[1]
    tn = 1024
    return pl.pallas_call(
        _mod_kernel,
        grid=(n // tn,),
        in_specs=[pl.BlockSpec((bsz, d), lambda j: (0, 0)),
                  pl.BlockSpec((d, tn), lambda j: (0, j)),
                  pl.BlockSpec((1, tn), lambda j: (0, j))],
        out_specs=pl.BlockSpec((bsz, tn), lambda j: (0, j)),
        out_shape=jax.ShapeDtypeStruct((bsz, n), F32),
        compiler_params=_cparams(("arbitrary",)),
        name="mod",
    )(c, w, b.reshape(1, n))


def _norm_mod(x, g, sc, sh):
    ms = jnp.mean(x * x, axis=-1, keepdims=True)
    return (x * lax.rsqrt(ms + EPS) * g) * (1.0 + sc) + sh


def _inproj_kernel(x_ref, mod_ref, g_ref, w_ref, proj_ref, f_ref, *, f_chunk, n_chunks, cw):
    h = _norm_mod(x_ref[0], g_ref[...], mod_ref[0, 1:2, :], mod_ref[0, 0:1, :]).astype(BF16)
    for c in range(n_chunks):
        r = jnp.dot(h, w_ref[:, c * cw:(c + 1) * cw], preferred_element_type=F32)
        proj_ref[0, :, c * cw:(c + 1) * cw] = r.astype(BF16)
        if c == f_chunk:
            f_ref[0] = r


def _inproj(x, mod6, g, w_bf):
    bsz, s, d = x.shape
    n = w_bf.shape[1]
    cw = 512
    tm = min(TM_PROJ, s)
    kern = functools.partial(_inproj_kernel, f_chunk=4, n_chunks=n // cw, cw=cw)
    return pl.pallas_call(
        kern,
        grid=(bsz, s // tm),
        in_specs=[pl.BlockSpec((1, tm, d), lambda b, i: (b, i, 0)),
                  pl.BlockSpec((1, 6, d), lambda b, i: (b, 0, 0)),
                  pl.BlockSpec((1, d), lambda b, i: (0, 0)),
                  pl.BlockSpec((d, n), lambda b, i: (0, 0))],
        out_specs=[pl.BlockSpec((1, tm, n), lambda b, i: (b, i, 0)),
                   pl.BlockSpec((1, tm, cw), lambda b, i: (b, i, 0))],
        out_shape=[jax.ShapeDtypeStruct((bsz, s, n), BF16),
                   jax.ShapeDtypeStruct((bsz, s, cw), F32)],
        compiler_params=_cparams(("arbitrary", "arbitrary")),
        name="inproj",
    )(x, mod6, g.reshape(1, d), w_bf)


LOG2E = math.log2(math.e)


def _attn_kernel(lq_ref, q_ref, k_ref, v_ref, d_ref, g_ref, o_ref,
                 qs_ref, vt_ref, s_ref, m_ref, l_ref, acc_ref, *, t, lam_init):
    qi = pl.program_id(2)
    nkt = vt_ref.shape[0]

    @pl.when(qi == 0)
    def _():
        for c in range(nkt):
            vt_ref[c] = v_ref[0, c * t:(c + 1) * t, :].T

    q = q_ref[0]
    lane = lax.broadcasted_iota(jnp.int32, q.shape, 1)
    qf = q.astype(F32) * (ATTN_HEAD_DIM ** -0.5 * LOG2E)
    zero = jnp.zeros_like(qf)
    qs_ref[0:t, :] = jnp.where(lane < ATTN_HEAD_DIM, qf, zero).astype(BF16)
    qs_ref[t:2 * t, :] = jnp.where(lane >= ATTN_HEAD_DIM, qf, zero).astype(BF16)
    m_ref[...] = jnp.full(m_ref.shape, NEG, F32)
    l_ref[...] = jnp.zeros(l_ref.shape, F32)
    acc_ref[...] = jnp.zeros(acc_ref.shape, F32)

    blocks = [slice(c * ATTN_COLS, (c + 1) * ATTN_COLS) for c in range(2 * t // ATTN_COLS)]

    def scores(j, cs):
        k = k_ref[0, pl.ds(pl.multiple_of(j * t, t), t), :]
        return lax.dot_general(k, qs_ref[cs, :], (((1,), (1,)), ((), ())), preferred_element_type=F32)

    def tile(j, bias_ref, j_next):
        vt = vt_ref[j]
        m_all = m_ref[...]
        l_all = l_ref[...]
        ps, alphas, m_news, l_news = [], [], [], []
        for cs in blocks:
            s = s_ref[:, cs]
            if bias_ref is not None:
                s = s + bias_ref[:, cs]
            m_old = m_all[:, cs]
            m_new = jnp.maximum(m_old, jnp.max(s, axis=0, keepdims=True))
            alpha = jnp.exp2(m_old - m_new)
            p = jnp.exp2(s - m_new)
            l_news.append(alpha * l_all[:, cs] + jnp.sum(p, axis=0, keepdims=True))
            ps.append(p.astype(BF16))
            alphas.append(alpha)
            m_news.append(m_new)
            if j_next is not None:
                s_ref[:, cs] = scores(j_next, cs)
        pvs = [jnp.dot(vt, p, preferred_element_type=F32) for p in ps]
        for cs, alpha, pv in zip(blocks, alphas, pvs):
            acc_ref[:, cs] = alpha * acc_ref[:, cs] + pv
        m_ref[...] = jnp.concatenate(m_news, axis=1)
        l_ref[...] = jnp.concatenate(l_news, axis=1)

    for cs in blocks:
        s_ref[:, cs] = scores(0, cs)

    def far(j, carry):
        tile(j, None, j + 1)
        return carry

    lax.fori_loop(0, qi - 1, far, 0)

    @pl.when(qi >= 1)
    def _():
        tile(qi - 1, d_ref.at[0, 1], qi)

    tile(qi, d_ref.at[0, 0], None)

    lq = lq_ref[...]
    lam = (jnp.exp(jnp.sum(lq[0:1] * lq[1:2], axis=-1, keepdims=True))
           - jnp.exp(jnp.sum(lq[2:3] * lq[3:4], axis=-1, keepdims=True)) + lam_init)
    o = acc_ref[...] / l_ref[...]
    a = o[:, 0:t] - lam * o[:, t:2 * t]
    ms = jnp.mean(a * a, axis=0, keepdims=True)
    y = (a * lax.rsqrt(ms + EPS)).T * (g_ref[...] * (1.0 - lam_init))
    o_ref[0] = y.astype(o_ref.dtype)


def _t5_bucket(rel):
    n = jnp.maximum(rel, 0)
    max_exact = NUM_BUCKETS // 2
    nf = jnp.maximum(n, 1).astype(F32)
    large = max_exact + (jnp.log(nf / max_exact) / math.log(MAX_DISTANCE / max_exact)
                         * (NUM_BUCKETS - max_exact)).astype(jnp.int32)
    large = jnp.minimum(large, NUM_BUCKETS - 1)
    return jnp.where(n < max_exact, n, large)


def _bias_tiles(rel_table, t):
    assert t >= MAX_DISTANCE
    nh = rel_table.shape[1]
    rel = jnp.arange(2 * t)
    vec = (rel_table[_t5_bucket(rel)].astype(F32) - rel_table[NUM_BUCKETS - 1].astype(F32)[None, :]).T * LOG2E
    neg = jnp.full((nh, t), NEG, F32)

    def toeplitz(w):
        flat = jnp.tile(w, (1, t))[:, :t * (2 * t - 1)]
        return flat.reshape(nh, t, 2 * t - 1)[:, :, :t]

    diag = toeplitz(jnp.concatenate([vec[:, :t], neg], axis=1))
    prev = toeplitz(jnp.concatenate([vec[:, t:], vec[:, :t]], axis=1))
    both = jnp.stack([diag, prev], axis=1)
    return jnp.concatenate([both, both], axis=-1)


def _attention(proj, lq4, subln_g, rel_table, layer):
    bsz, s, _ = proj.shape
    t = min(T_ATTN, s)
    nq = s // t
    lam_init = 0.8 - 0.6 * math.exp(-0.3 * layer)
    dt = _bias_tiles(rel_table, t)
    hq, hk, hv = 0, ATTN_HEADS, 2 * ATTN_HEADS
    kern = functools.partial(_attn_kernel, t=t, lam_init=lam_init)
    return pl.pallas_call(
        kern,
        grid=(bsz, ATTN_HEADS, nq),
        in_specs=[pl.BlockSpec((4, ATTN_HEAD_DIM), lambda b, h, i: (0, 0)),
                  pl.BlockSpec((1, t, HEAD_W), lambda b, h, i: (b, i, hq + h)),
                  pl.BlockSpec((1, s, HEAD_W), lambda b, h, i: (b, 0, hk + h)),
                  pl.BlockSpec((1, s, HEAD_W), lambda b, h, i: (b, 0, hv + h)),
                  pl.BlockSpec((1, 2, t, 2 * t), lambda b, h, i: (h, 0, 0, 0)),
                  pl.BlockSpec((1, HEAD_W), lambda b, h, i: (0, 0))],
        out_specs=pl.BlockSpec((1, t, HEAD_W), lambda b, h, i: (b, i, h)),
        out_shape=jax.ShapeDtypeStruct((bsz, s, ATTN_HEADS * HEAD_W), BF16),
        scratch_shapes=[pltpu.VMEM((2 * t, HEAD_W), BF16),
                        pltpu.VMEM((nq, HEAD_W, t), BF16),
                        pltpu.VMEM((t, 2 * t), F32),
                        pltpu.VMEM((1, 2 * t), F32),
                        pltpu.VMEM((1, 2 * t), F32),
                        pltpu.VMEM((HEAD_W, 2 * t), F32)],
        compiler_params=_cparams(("arbitrary", "arbitrary", "arbitrary")),
        name="attn",
    )(lq4, proj, proj, proj, dt, subln_g.reshape(1, HEAD_W))


def _hgrn_kernel(lb_ref, q_ref, f_ref, i_ref, g_ref, ng_ref, o_ref, st_ref, *, tr, layer):
    c = REC_CHUNK
    nch = tr // c

    @pl.when(pl.program_id(2) == 0)
    def _():
        st_ref[...] = jnp.zeros(st_ref.shape, F32)

    lbr = lb_ref[...]
    e = jnp.exp(lbr - jnp.max(lbr, axis=0, keepdims=True))
    lb = jnp.sum(e[0:layer + 1], axis=0, keepdims=True) / jnp.sum(e, axis=0, keepdims=True)

    fr = f_ref[0]
    sg = jax.nn.sigmoid(fr)
    lf = jnp.log(lb + (1.0 - lb) * sg)
    kf = (1.0 - lb) * jax.nn.sigmoid(-fr)
    qf = _silu(q_ref[0].astype(F32))
    vb = i_ref[0]

    row = lax.broadcasted_iota(jnp.int32, (tr, LANES), 0)
    rin = row % c

    b = lf
    sh = 1
    while sh < c:
        b = b + jnp.where(rin >= sh, pltpu.roll(b, sh, axis=0), 0.0)
        sh *= 2

    r8 = row % 8
    diags = []
    for dlt in range(8):
        if dlt == 0:
            term = qf * kf
        else:
            kd = pltpu.roll(kf, dlt, axis=0)
            bd = pltpu.roll(b, dlt, axis=0)
            term = jnp.where(r8 >= dlt, qf * kd * jnp.exp(jnp.minimum(b - bd, 0.0)), 0.0)
        diags.append(jnp.sum(term, axis=-1, keepdims=True))
    lv_q, lv_k = [], []
    for w in (8, 16, 32):
        bend = jnp.broadcast_to(b.reshape(tr // w, w, LANES)[:, w - 1:w, :],
                                (tr // w, w, LANES)).reshape(tr, LANES)
        right = (row // w) % 2 == 1
        pend = pltpu.roll(bend, w, axis=0)
        lv_q.append(jnp.where(right, qf * jnp.exp(jnp.minimum(b - pend, 0.0)), 0.0).astype(BF16))
        lv_k.append(jnp.where(right, 0.0, kf * jnp.exp(jnp.minimum(bend - b, 0.0))).astype(BF16))

    blast = jnp.broadcast_to(b.reshape(nch, c, LANES)[:, c - 1:c, :], (nch, c, LANES)).reshape(tr, LANES)
    qdec = (qf * jnp.exp(b)).astype(BF16)
    kdec = (kf * jnp.exp(blast - b)).astype(BF16)
    dec = jnp.exp(blast)

    ri = lax.broadcasted_iota(jnp.int32, (c, c), 0)
    ci = lax.broadcasted_iota(jnp.int32, (c, c), 1)
    nt = (((1,), (1,)), ((), ()))
    outs = []
    for ch in range(nch):
        sl = slice(ch * c, (ch + 1) * c)
        a = jnp.zeros((c, c), F32)
        for li, w in enumerate((8, 16, 32)):
            sw = lax.dot_general(lv_q[li][sl], lv_k[li][sl], nt, preferred_element_type=F32)
            a = a + jnp.where(ri // (2 * w) == ci // (2 * w), sw, 0.0)
        for dlt in range(8):
            a = a + jnp.where(ci == ri - dlt, diags[dlt][sl], 0.0)
        st = st_ref[...]
        o = lax.dot_general(qdec[sl], st.astype(BF16), nt, preferred_element_type=F32)
        o = o + jnp.dot(a.astype(BF16), vb[sl], preferred_element_type=F32)
        outs.append(o)
        upd = lax.dot_general(vb[sl], kdec[sl], (((0,), (0,)), ((), ())), preferred_element_type=F32)
        st_ref[...] = dec[ch * c:ch * c + 1, :] * st + upd

    o = jnp.concatenate(outs, axis=0)
    ms = jnp.mean(o * o, axis=-1, keepdims=True)
    y = (o * lax.rsqrt(ms + EPS) * ng_ref[...]) * _silu(g_ref[0].astype(F32))
    o_ref[0] = y.astype(o_ref.dtype)


def _hgrn(proj, f32f, lb_raw, norm_g, layer):
    bsz, s, _ = proj.shape
    tr = min(T_REC, s)
    nl = lb_raw.shape[0]
    cq, ci, cg = 12, 20, 24
    kern = functools.partial(_hgrn_kernel, tr=tr, layer=layer)
    return pl.pallas_call(
        kern,
        grid=(bsz, REC_HEADS, s // tr),
        in_specs=[pl.BlockSpec((nl, HEAD_W), lambda b, h, i: (0, h)),
                  pl.BlockSpec((1, tr, HEAD_W), lambda b, h, i: (b, i, cq + h)),
                  pl.BlockSpec((1, tr, HEAD_W), lambda b, h, i: (b, i, h)),
                  pl.BlockSpec((1, tr, HEAD_W), lambda b, h, i: (b, i, ci + h)),
                  pl.BlockSpec((1, tr, HEAD_W), lambda b, h, i: (b, i, cg + h)),
                  pl.BlockSpec((1, HEAD_W), lambda b, h, i: (0, h))],
        out_specs=pl.BlockSpec((1, tr, HEAD_W), lambda b, h, i: (b, i, h)),
        out_shape=jax.ShapeDtypeStruct((bsz, s, REC_HEADS * HEAD_W), BF16),
        scratch_shapes=[pltpu.VMEM((HEAD_W, HEAD_W), F32)],
        compiler_params=_cparams(("arbitrary", "arbitrary", "arbitrary")),
        name="hgrn",
    )(lb_raw, proj, f32f, proj, proj, norm_g.reshape(1, -1))


def _pack_bf16_pairs(x):
    n = x.shape[1] // 2
    lo = pltpu.bitcast(x[:, :n].astype(BF16).astype(F32), jnp.uint32)
    hi = pltpu.bitcast(x[:, n:].astype(BF16).astype(F32), jnp.uint32)
    return (hi & jnp.uint32(0xFFFF0000)) | (lo >> 16)


def _join_bf16_pairs(lo, hi):
    return pltpu.bitcast(hi, jnp.uint32) | (pltpu.bitcast(lo, jnp.uint32) >> 16)


def _unpack_bf16_pairs(w):
    lo = pltpu.bitcast(w << 16, F32)
    hi = pltpu.bitcast(w & jnp.uint32(0xFFFF0000), F32)
    return lo, hi


def _outproj_kernel(ya_ref, yr_ref, x_ref, mod_ref, g_ref, wo_ref, wr_ref, br_ref,
                    x1_ref, h2_ref, meta_ref, cnt_ref, run_ref, *, tm):
    first = (pl.program_id(0) == 0) & (pl.program_id(1) == 0)

    @pl.when(first)
    def _():
        run_ref[...] = jnp.zeros(run_ref.shape, F32)

    half = ya_ref.shape[2]
    mix = (jnp.dot(ya_ref[0], wo_ref[0:half, :], preferred_element_type=F32)
           + jnp.dot(yr_ref[0], wo_ref[half:2 * half, :], preferred_element_type=F32))
    x1 = x_ref[0] + mod_ref[0, 2:3, :] * mix
    x1_ref[0] = x1
    h2 = _norm_mod(x1, g_ref[...], mod_ref[0, 4:5, :], mod_ref[0, 3:4, :])
    h2_ref[0] = _pack_bf16_pairs(h2)

    logits = jnp.dot(h2.astype(BF16), wr_ref[...], preferred_element_type=F32) + br_ref[...]
    lane = lax.broadcasted_iota(jnp.int32, logits.shape, 1)
    lane_f = lane.astype(F32)

    def first_lane(mask):
        return jnp.min(jnp.where(mask, lane_f, float(LANES)), axis=-1, keepdims=True).astype(jnp.int32)

    is_g = lane < N_GROUPS
    gl = jnp.where(is_g, logits, NEG)
    gmax = jnp.max(gl, axis=-1, keepdims=True)
    gsum = jnp.sum(jnp.exp(gl - gmax), axis=-1, keepdims=True)
    g_gate = 1.0 / gsum
    g_idx = first_lane(is_g & (gl == gmax))
    e_lo = N_GROUPS + g_idx * EXPERTS_PER_GROUP
    in_grp = (lane >= e_lo) & (lane < e_lo + EXPERTS_PER_GROUP)
    el = jnp.where(in_grp, logits, NEG)
    m1 = jnp.max(el, axis=-1, keepdims=True)
    i1 = first_lane(in_grp & (el == m1))
    el2 = jnp.where(lane == i1, NEG, el)
    m2 = jnp.max(el2, axis=-1, keepdims=True)
    i2 = first_lane(in_grp & (el2 == m2))
    r = jnp.exp(m2 - m1)
    w1 = g_gate / (1.0 + r)
    w2 = g_gate * r / (1.0 + r)
    e1 = i1 - N_GROUPS
    e2 = i2 - N_GROUPS

    oh1 = (lane == e1).astype(F32)
    oh2 = (lane == e2).astype(F32)
    oh = oh1 + oh2
    ri = lax.broadcasted_iota(jnp.int32, (tm, tm), 0)
    ci = lax.broadcasted_iota(jnp.int32, (tm, tm), 1)
    tri = (ci < ri).astype(BF16)
    pre = jnp.dot(tri, oh.astype(BF16), preferred_element_type=F32) + run_ref[0:1, :]
    rk1 = jnp.sum(pre * oh1, axis=-1, keepdims=True)
    rk2 = jnp.sum(pre * oh2, axis=-1, keepdims=True)
    vals = (e1.astype(F32), e2.astype(F32), w1, w2, rk1, rk2)
    meta = jnp.zeros(logits.shape, F32)
    for i, v in enumerate(vals):
        meta = jnp.where(lane == i, v, meta)
    meta_ref[0] = meta
    run_ref[0:1, :] = run_ref[0:1, :] + jnp.sum(oh, axis=0, keepdims=True)
    cnt_ref[...] = run_ref[...]


def _outproj(ya, yr, x, mod6, g2n, wo_bf, wr_bf, br):
    bsz, s, d = x.shape
    tm = min(TM_PROJ, s)
    half = ya.shape[2]
    kern = functools.partial(_outproj_kernel, tm=tm)
    tok = lambda b, i: (b, i, 0)
    return pl.pallas_call(
        kern,
        grid=(bsz, s // tm),
        in_specs=[pl.BlockSpec((1, tm, half), tok),
                  pl.BlockSpec((1, tm, half), tok),
                  pl.BlockSpec((1, tm, d), tok),
                  pl.BlockSpec((1, 6, d), lambda b, i: (b, 0, 0)),
                  pl.BlockSpec((1, d), lambda b, i: (0, 0)),
                  pl.BlockSpec((2 * half, d), lambda b, i: (0, 0)),
                  pl.BlockSpec((d, LANES), lambda b, i: (0, 0)),
                  pl.BlockSpec((1, LANES), lambda b, i: (0, 0))],
        out_specs=[pl.BlockSpec((1, tm, d), tok),
                   pl.BlockSpec((1, tm, d // 2), tok),
                   pl.BlockSpec((1, tm, LANES), tok),
                   pl.BlockSpec((8, LANES), lambda b, i: (0, 0))],
        out_shape=[jax.ShapeDtypeStruct((bsz, s, d), F32),
                   jax.ShapeDtypeStruct((bsz, s, d // 2), jnp.uint32),
                   jax.ShapeDtypeStruct((bsz, s, LANES), F32),
                   jax.ShapeDtypeStruct((8, LANES), F32)],
        scratch_shapes=[pltpu.VMEM((8, LANES), F32)],
        compiler_params=_cparams(("arbitrary", "arbitrary")),
        name="outproj",
    )(ya, yr, x, mod6, g2n.reshape(1, d), wo_bf, wr_bf, br)


ROW_UNROLL = 8


def _dispatch_kernel(dest_ref, h_ref, rows_ref, sem, *, tm):
    def start(n, carry):
        for kk in range(TOP_K):
            d = dest_ref[n * TOP_K + kk]
            pltpu.make_async_copy(h_ref.at[pl.ds(n, 1)], rows_ref.at[pl.ds(d, 1)], sem).start(priority=kk)
        return carry

    lax.fori_loop(0, tm, start, 0, unroll=ROW_UNROLL)
    for kk in range(TOP_K):
        pltpu.make_async_copy(h_ref, rows_ref.at[pl.ds(0, tm)], sem).wait()


def _dispatch(dest, h2p):
    n, dw = h2p.shape
    tm = min(TM_PROJ, n)
    kern = functools.partial(_dispatch_kernel, tm=tm)
    return pl.pallas_call(
        kern,
        grid=(n // tm,),
        in_specs=[pl.BlockSpec((tm * TOP_K,), lambda i: (i,), memory_space=pltpu.SMEM),
                  pl.BlockSpec((tm, dw), lambda i: (i, 0))],
        out_specs=pl.BlockSpec(memory_space=pl.ANY),
        scratch_shapes=[pltpu.SemaphoreType.DMA(())],
        out_shape=jax.ShapeDtypeStruct((n * TOP_K, dw), jnp.uint32),
        compiler_params=_cparams(("arbitrary",)),
        name="dispatch",
    )(dest, h2p)


def _expert_kernel(wblk_ref, wexp_ref, wlo_ref, whi_ref, rows_ref, w1_ref, w3_ref, w2_ref, y_ref,
                   w1b_ref, w3b_ref, w2b_ref, acc_ref, *, blk, n_work):
    w = pl.program_id(0)
    prev = jnp.maximum(w - 1, 0)
    nxt = jnp.minimum(w + 1, n_work - 1)
    new_exp = (w == 0) | (wexp_ref[w] != wexp_ref[prev])
    new_blk = (w == 0) | (wblk_ref[w] != wblk_ref[prev])
    last_of_blk = (w == n_work - 1) | (wblk_ref[w] != wblk_ref[nxt])
    lo = wlo_ref[w]
    hi = whi_ref[w]

    @pl.when(new_exp)
    def _():
        w1b_ref[...] = w1_ref[0].astype(BF16)
        w3b_ref[...] = w3_ref[0].astype(BF16)
        w2b_ref[...] = w2_ref[0].astype(BF16)

    @pl.when(new_blk)
    def _():
        acc_ref[...] = jnp.zeros(acc_ref.shape, F32)

    @pl.when(hi > lo)
    def _():
        xlo, xhi = _unpack_bf16_pairs(rows_ref[...])
        half = xlo.shape[1]
        xlo = xlo.astype(BF16)
        xhi = xhi.astype(BF16)
        a = (jnp.dot(xlo, w1b_ref[0:half, :], preferred_element_type=F32)
             + jnp.dot(xhi, w1b_ref[half:2 * half, :], preferred_element_type=F32))
        b = (jnp.dot(xlo, w3b_ref[0:half, :], preferred_element_type=F32)
             + jnp.dot(xhi, w3b_ref[half:2 * half, :], preferred_element_type=F32))
        hmid = (_silu(a) * b).astype(BF16)
        y = jnp.dot(hmid, w2b_ref[...], preferred_element_type=F32)
        row = lax.broadcasted_iota(jnp.int32, (blk, 1), 0)
        keep = (row >= lo) & (row < hi)
        acc_ref[...] = acc_ref[...] + jnp.where(keep, y, 0.0)

    @pl.when(last_of_blk)
    def _():
        y_ref[...] = _pack_bf16_pairs(acc_ref[...])


def _experts(wblk, wexp, wlo, whi, rows, w1, w3, w2):
    a_rows, dw = rows.shape
    n_work = wblk.shape[0]
    blk = min(BLK_E, a_rows)
    _, d, ff = w1.shape
    kern = functools.partial(_expert_kernel, blk=blk, n_work=n_work)
    return pl.pallas_call(
        kern,
        grid_spec=pltpu.PrefetchScalarGridSpec(
            num_scalar_prefetch=4,
            grid=(n_work,),
            in_specs=[pl.BlockSpec((blk, dw), lambda w, wb, we, wl, wh: (wb[w], 0)),
                      pl.BlockSpec((1, d, ff), lambda w, wb, we, wl, wh: (we[w], 0, 0)),
                      pl.BlockSpec((1, d, ff), lambda w, wb, we, wl, wh: (we[w], 0, 0)),
                      pl.BlockSpec((1, ff, d), lambda w, wb, we, wl, wh: (we[w], 0, 0))],
            out_specs=pl.BlockSpec((blk, dw), lambda w, wb, we, wl, wh: (wb[w], 0)),
            scratch_shapes=[pltpu.VMEM((d, ff), BF16),
                            pltpu.VMEM((d, ff), BF16),
                            pltpu.VMEM((ff, d), BF16),
                            pltpu.VMEM((blk, d), F32)]),
        out_shape=jax.ShapeDtypeStruct((a_rows, dw), jnp.uint32),
        compiler_params=_cparams(("arbitrary",)),
        name="experts",
    )(wblk, wexp, wlo, whi, rows, w1, w3, w2)


def _work_items(counts, a_rows, blk):
    n_blk = a_rows // blk
    ends = jnp.cumsum(counts)
    starts = ends - counts
    pts = jnp.sort(jnp.concatenate([jnp.arange(n_blk, dtype=jnp.int32) * blk, starts[1:]]))
    nxt = jnp.concatenate([pts[1:], jnp.array([a_rows], jnp.int32)])
    wblk = jnp.minimum(pts // blk, n_blk - 1)
    wexp = jnp.minimum(jnp.sum(ends[None, :] <= pts[:, None], axis=1), N_EXPERTS - 1).astype(jnp.int32)
    lo = pts - wblk * blk
    hi = jnp.minimum(nxt, (wblk + 1) * blk) - wblk * blk
    hi = jnp.maximum(hi, lo)
    return wblk.astype(jnp.int32), wexp, lo.astype(jnp.int32), hi.astype(jnp.int32)


def _combine_kernel(dest_ref, y_ref, x1_ref, meta_ref, mod_ref, g_ref, o_ref, buf_ref, sem, *, tm):
    def start(n, carry):
        for kk in range(TOP_K):
            d = dest_ref[n * TOP_K + kk]
            pltpu.make_async_copy(y_ref.at[pl.ds(d, 1)], buf_ref.at[kk, pl.ds(n, 1)], sem).start(priority=kk)
        return carry

    lax.fori_loop(0, tm, start, 0, unroll=ROW_UNROLL)
    for kk in range(TOP_K):
        pltpu.make_async_copy(y_ref.at[pl.ds(0, tm)], buf_ref.at[kk], sem).wait()

    meta = meta_ref[0]
    lo0, hi0 = _unpack_bf16_pairs(buf_ref[0])
    lo1, hi1 = _unpack_bf16_pairs(buf_ref[1])
    w0 = meta[:, 2:3]
    w1 = meta[:, 3:4]
    moe = jnp.concatenate([w0 * lo0 + w1 * lo1, w0 * hi0 + w1 * hi1], axis=1)
    x2 = x1_ref[0] + mod_ref[0, 5:6, :] * moe
    ms = jnp.mean(x2 * x2, axis=-1, keepdims=True)
    o_ref[0] = x2 * lax.rsqrt(ms + EPS) * g_ref[...]


def _combine(dest, ybuf, x1, meta, mod6, gf):
    bsz, s, d = x1.shape
    tm = min(TM_PROJ, s)
    kern = functools.partial(_combine_kernel, tm=tm)
    tok = lambda b, i: (b, i, 0)
    nt = s // tm
    return pl.pallas_call(
        kern,
        grid=(bsz, nt),
        in_specs=[pl.BlockSpec((tm * TOP_K,), lambda b, i: (b * nt + i,), memory_space=pltpu.SMEM),
                  pl.BlockSpec(memory_space=pl.ANY),
                  pl.BlockSpec((1, tm, d), tok),
                  pl.BlockSpec((1, tm, LANES), tok),
                  pl.BlockSpec((1, 6, d), lambda b, i: (b, 0, 0)),
                  pl.BlockSpec((1, d), lambda b, i: (0, 0))],
        out_specs=pl.BlockSpec((1, tm, d), tok),
        scratch_shapes=[pltpu.VMEM((TOP_K, tm, d // 2), jnp.uint32),
                        pltpu.SemaphoreType.DMA(())],
        out_shape=jax.ShapeDtypeStruct((bsz, s, d), F32),
        compiler_params=_cparams(("arbitrary", "arbitrary")),
        name="combine",
    )(dest, ybuf, x1, meta, mod6, gf.reshape(1, d))


def kernel(x, c, w_ada, b_ada, norm1_g, norm2_g, w_in, attn_lambda_q1, attn_lambda_k1, attn_lambda_q2,
           attn_lambda_k2, attn_subln_g, rel_bias_table, rec_lower_bound, rec_norm_g, w_out, w_group,
           b_group, w_expert, b_expert, w1, w3, w2, final_norm_g):
    bsz, s, d = x.shape
    depth = w_ada.shape[0]
    n_tok = bsz * s
    for l in range(depth):
        mod6 = _mod(c, w_ada[l], b_ada[l]).reshape(bsz, 6, d)
        proj, f32f = _inproj(x, mod6, norm1_g[l], w_in[l].astype(BF16))
        lq4 = jnp.stack([attn_lambda_q1[l], attn_lambda_k1[l], attn_lambda_q2[l], attn_lambda_k2[l]])
        ya = _attention(proj, lq4, attn_subln_g[l], rel_bias_table, l)
        yr = _hgrn(proj, f32f, rec_lower_bound, rec_norm_g[l], l)
        w_r = jnp.zeros((d, LANES), F32).at[:, :N_GROUPS].set(w_group[l])
        w_r = w_r.at[:, N_GROUPS:N_GROUPS + N_EXPERTS].set(w_expert[l]).astype(BF16)
        b_r = jnp.zeros((1, LANES), F32).at[0, :N_GROUPS].set(b_group[l])
        b_r = b_r.at[0, N_GROUPS:N_GROUPS + N_EXPERTS].set(b_expert[l])
        x1, h2p, meta, cnt = _outproj(ya, yr, x, mod6, norm2_g[l], w_out[l].astype(BF16), w_r, b_r)
        counts = cnt[0, :N_EXPERTS].astype(jnp.int32)
        starts = jnp.cumsum(counts) - counts
        eid = meta[..., 0:TOP_K].astype(jnp.int32)
        rank = meta[..., 4:4 + TOP_K].astype(jnp.int32)
        eoh = eid[..., None] == jnp.arange(N_EXPERTS, dtype=jnp.int32)
        dest = (jnp.sum(jnp.where(eoh, starts, 0), axis=-1) + rank).reshape(n_tok * TOP_K)
        rows = _dispatch(dest, h2p.reshape(n_tok, d // 2))
        wblk, wexp, wlo, whi = _work_items(counts, n_tok * TOP_K, min(BLK_E, n_tok * TOP_K))
        ybuf = _experts(wblk, wexp, wlo, whi, rows, w1[l], w3[l], w2[l])
        gf = final_norm_g if l == depth - 1 else jnp.ones((d,), F32)
        assert depth == 1
        x = _combine(dest, ybuf, x1, meta, mod6, gf)
    return x
```

```python
import functools
import math

import jax
import jax.numpy as jnp
from jax import lax
from jax.experimental import pallas as pl
from jax.experimental.pallas import tpu as pltpu

F32 = jnp.float32
BF16 = jnp.bfloat16
EPS = 1e-6

ATTN_HEADS = 4
ATTN_HEAD_DIM = 64
HEAD_W = 128
REC_HEADS = 4
REC_CHUNK = 64
NUM_BUCKETS = 32
MAX_DISTANCE = 128
N_GROUPS = 4
EXPERTS_PER_GROUP = 8
N_EXPERTS = 32
TOP_K = 2
NEG = -1e30

LANES = 128
VMEM_LIMIT = 56 * 1024 * 1024

TM_PROJ = 512
T_ATTN = 512
ATTN_COLS = 256
T_REC = 512
REC_TRI = 256
BLK_E = 512


def _cparams(sem):
    return pltpu.CompilerParams(dimension_semantics=sem, vmem_limit_bytes=VMEM_LIMIT)


def _sigmoid(x):
    return 0.5 * jnp.tanh(0.5 * x) + 0.5


def _silu(x):
    return x * _sigmoid(x)


def _mod_kernel(c_ref, w_ref, b_ref, o_ref):
    ca = _silu(c_ref[...])
    o_ref[...] = jnp.dot(ca, w_ref[...], preferred_element_type=F32,
                         precision=lax.Precision.HIGHEST) + b_ref[...]


def _mod(c, w, b):
    bsz, d = c.shape
    n = w.shape[1]
    tn = 1024
    return pl.pallas_call(
        _mod_kernel,
        grid=(n // tn,),
        in_specs=[pl.BlockSpec((bsz, d), lambda j: (0, 0)),
                  pl.BlockSpec((d, tn), lambda j: (0, j)),
                  pl.BlockSpec((1, tn), lambda j: (0, j))],
        out_specs=pl.BlockSpec((bsz, tn), lambda j: (0, j)),
        out_shape=jax.ShapeDtypeStruct((bsz, n), F32),
        compiler_params=_cparams(("arbitrary",)),
        name="mod",
    )(c, w, b.reshape(1, n))


def _norm_mod(x, g, sc, sh):
    ms = jnp.mean(x * x, axis=-1, keepdims=True)
    return (x * lax.rsqrt(ms + EPS) * g) * (1.0 + sc) + sh


def _inproj_kernel(x_ref, mod_ref, g_ref, w_ref, proj_ref, f_ref, *, f_chunk, n_chunks, cw):
    h = _norm_mod(x_ref[0], g_ref[...], mod_ref[0, 1:2, :], mod_ref[0, 0:1, :]).astype(BF16)
    for c in range(n_chunks):
        r = jnp.dot(h, w_ref[:, c * cw:(c + 1) * cw], preferred_element_type=F32)
        proj_ref[0, :, c * cw:(c + 1) * cw] = r.astype(BF16)
        if c == f_chunk:
            f_ref[0] = r


def _inproj(x, mod6, g, w_bf):
    bsz, s, d = x.shape
    n = w_bf.shape[1]
    cw = 512
    tm = min(TM_PROJ, s)
    kern = functools.partial(_inproj_kernel, f_chunk=4, n_chunks=n // cw, cw=cw)
    return pl.pallas_call(
        kern,
        grid=(bsz, s // tm),
        in_specs=[pl.BlockSpec((1, tm, d), lambda b, i: (b, i, 0)),
                  pl.BlockSpec((1, 6, d), lambda b, i: (b, 0, 0)),
                  pl.BlockSpec((1, d), lambda b, i: (0, 0)),
                  pl.BlockSpec((d, n), lambda b, i: (0, 0))],
        out_specs=[pl.BlockSpec((1, tm, n), lambda b, i: (b, i, 0)),
                   pl.BlockSpec((1, tm, cw), lambda b, i: (b, i, 0))],
        out_shape=[jax.ShapeDtypeStruct((bsz, s, n), BF16),
                   jax.ShapeDtypeStruct((bsz, s, cw), F32)],
        compiler_params=_cparams(("arbitrary", "arbitrary")),
        name="inproj",
    )(x, mod6, g.reshape(1, d), w_bf)


LOG2E = math.log2(math.e)


def _attn_kernel(lq_ref, q_ref, k_ref, v_ref, d_ref, g_ref, o_ref,
                 qs_ref, vt_ref, s_ref, m_ref, l_ref, acc_ref, *, t, lam_init):
    qi = pl.program_id(2)
    nkt = vt_ref.shape[0]

    @pl.when(qi == 0)
    def _():
        for c in range(nkt):
            vt_ref[c] = v_ref[0, c * t:(c + 1) * t, :].T

    q = q_ref[0]
    lane = lax.broadcasted_iota(jnp.int32, q.shape, 1)
    qf = q.astype(F32) * (ATTN_HEAD_DIM ** -0.5 * LOG2E)
    zero = jnp.zeros_like(qf)
    qs_ref[0:t, :] = jnp.where(lane < ATTN_HEAD_DIM, qf, zero).astype(BF16)
    qs_ref[t:2 * t, :] = jnp.where(lane >= ATTN_HEAD_DIM, qf, zero).astype(BF16)
    m_ref[...] = jnp.full(m_ref.shape, NEG, F32)
    l_ref[...] = jnp.zeros(l_ref.shape, F32)
    acc_ref[...] = jnp.zeros(acc_ref.shape, F32)

    blocks = [slice(c * ATTN_COLS, (c + 1) * ATTN_COLS) for c in range(2 * t // ATTN_COLS)]

    def scores(j, cs):
        k = k_ref[0, pl.ds(pl.multiple_of(j * t, t), t), :]
        return lax.dot_general(k, qs_ref[cs, :], (((1,), (1,)), ((), ())), preferred_element_type=F32)

    def tile(j, bias_ref, j_next):
        vt = vt_ref[j]
        m_all = m_ref[...]
        l_all = l_ref[...]
        ps, alphas, m_news, l_news = [], [], [], []
        for cs in blocks:
            s = s_ref[:, cs]
            if bias_ref is not None:
                s = s + bias_ref[:, cs]
            m_old = m_all[:, cs]
            m_new = jnp.maximum(m_old, jnp.max(s, axis=0, keepdims=True))
            alpha = jnp.exp2(m_old - m_new)
            p = jnp.exp2(s - m_new)
            l_news.append(alpha * l_all[:, cs] + jnp.sum(p, axis=0, keepdims=True))
            ps.append(p.astype(BF16))
            alphas.append(alpha)
            m_news.append(m_new)
            if j_next is not None:
                s_ref[:, cs] = scores(j_next, cs)
        pvs = [jnp.dot(vt, p, preferred_element_type=F32) for p in ps]
        for cs, alpha, pv in zip(blocks, alphas, pvs):
            acc_ref[:, cs] = alpha * acc_ref[:, cs] + pv
        m_ref[...] = jnp.concatenate(m_news, axis=1)
        l_ref[...] = jnp.concatenate(l_news, axis=1)

    for cs in blocks:
        s_ref[:, cs] = scores(0, cs)

    n_far = jnp.maximum(qi - 1, 0)

    def far_pair(jj, carry):
        tile(2 * jj, None, 2 * jj + 1)
        tile(2 * jj + 1, None, 2 * jj + 2)
        return carry

    lax.fori_loop(0, n_far // 2, far_pair, 0)

    @pl.when(n_far % 2 == 1)
    def _():
        tile(n_far - 1, None, n_far)

    @pl.when(qi >= 1)
    def _():
        tile(qi - 1, d_ref.at[0, 1], qi)

    tile(qi, d_ref.at[0, 0], None)

    lq = lq_ref[...]
    lam = (jnp.exp(jnp.sum(lq[0:1] * lq[1:2], axis=-1, keepdims=True))
           - jnp.exp(jnp.sum(lq[2:3] * lq[3:4], axis=-1, keepdims=True)) + lam_init)
    o = acc_ref[...] / l_ref[...]
    a = o[:, 0:t] - lam * o[:, t:2 * t]
    ms = jnp.mean(a * a, axis=0, keepdims=True)
    y = (a * lax.rsqrt(ms + EPS)).T * (g_ref[...] * (1.0 - lam_init))
    o_ref[0] = y.astype(o_ref.dtype)


def _t5_bucket(rel):
    n = jnp.maximum(rel, 0)
    max_exact = NUM_BUCKETS // 2
    nf = jnp.maximum(n, 1).astype(F32)
    large = max_exact + (jnp.log(nf / max_exact) / math.log(MAX_DISTANCE / max_exact)
                         * (NUM_BUCKETS - max_exact)).astype(jnp.int32)
    large = jnp.minimum(large, NUM_BUCKETS - 1)
    return jnp.where(n < max_exact, n, large)


def _bias_tiles(rel_table, t):
    assert t >= MAX_DISTANCE
    nh = rel_table.shape[1]
    rel = jnp.arange(2 * t)
    vec = (rel_table[_t5_bucket(rel)].astype(F32) - rel_table[NUM_BUCKETS - 1].astype(F32)[None, :]).T * LOG2E
    neg = jnp.full((nh, t), NEG, F32)

    def toeplitz(w):
        flat = jnp.tile(w, (1, t))[:, :t * (2 * t - 1)]
        return flat.reshape(nh, t, 2 * t - 1)[:, :, :t]

    diag = toeplitz(jnp.concatenate([vec[:, :t], neg], axis=1))
    prev = toeplitz(jnp.concatenate([vec[:, t:], vec[:, :t]], axis=1))
    both = jnp.stack([diag, prev], axis=1)
    return jnp.concatenate([both, both], axis=-1)


def _attention(proj, lq4, subln_g, rel_table, layer):
    bsz, s, _ = proj.shape
    t = min(T_ATTN, s)
    nq = s // t
    lam_init = 0.8 - 0.6 * math.exp(-0.3 * layer)
    dt = _bias_tiles(rel_table, t)
    hq, hk, hv = 0, ATTN_HEADS, 2 * ATTN_HEADS
    kern = functools.partial(_attn_kernel, t=t, lam_init=lam_init)
    return pl.pallas_call(
        kern,
        grid=(bsz, ATTN_HEADS, nq),
        in_specs=[pl.BlockSpec((4, ATTN_HEAD_DIM), lambda b, h, i: (0, 0)),
                  pl.BlockSpec((1, t, HEAD_W), lambda b, h, i: (b, i, hq + h)),
                  pl.BlockSpec((1, s, HEAD_W), lambda b, h, i: (b, 0, hk + h)),
                  pl.BlockSpec((1, s, HEAD_W), lambda b, h, i: (b, 0, hv + h)),
                  pl.BlockSpec((1, 2, t, 2 * t), lambda b, h, i: (h, 0, 0, 0)),
                  pl.BlockSpec((1, HEAD_W), lambda b, h, i: (0, 0))],
        out_specs=pl.BlockSpec((1, t, HEAD_W), lambda b, h, i: (b, i, h)),
        out_shape=jax.ShapeDtypeStruct((bsz, s, ATTN_HEADS * HEAD_W), BF16),
        scratch_shapes=[pltpu.VMEM((2 * t, HEAD_W), BF16),
                        pltpu.VMEM((nq, HEAD_W, t), BF16),
                        pltpu.VMEM((t, 2 * t), F32),
                        pltpu.VMEM((1, 2 * t), F32),
                        pltpu.VMEM((1, 2 * t), F32),
                        pltpu.VMEM((HEAD_W, 2 * t), F32)],
        compiler_params=_cparams(("arbitrary", "arbitrary", "arbitrary")),
        name="attn",
    )(lq4, proj, proj, proj, dt, subln_g.reshape(1, HEAD_W))


def _hgrn_kernel(lb_ref, tri_ref, q_ref, f_ref, i_ref, g_ref, ng_ref, o_ref, st_ref, *, tr, layer):
    c = REC_CHUNK
    nch = tr // c

    @pl.when(pl.program_id(2) == 0)
    def _():
        st_ref[...] = jnp.zeros(st_ref.shape, F32)

    lbr = lb_ref[...]
    e = jnp.exp(lbr - jnp.max(lbr, axis=0, keepdims=True))
    lb = jnp.sum(e[0:layer + 1], axis=0, keepdims=True) / jnp.sum(e, axis=0, keepdims=True)

    fr = f_ref[0]
    sg = _sigmoid(fr)
    lf = jnp.log(lb + (1.0 - lb) * sg)
    kf = (1.0 - lb) * (1.0 - sg)
    qf = _silu(q_ref[0].astype(F32))
    vb = i_ref[0]

    row = lax.broadcasted_iota(jnp.int32, (tr, LANES), 0)

    lf_hi = lf.astype(BF16)
    lf_lo = (lf - lf_hi.astype(F32)).astype(BF16)
    lf2 = jnp.concatenate([lf_hi, lf_lo], axis=1)
    tw = tri_ref.shape[0]
    parts = []
    for r0 in range(0, tr, tw):
        cs2 = jnp.dot(tri_ref[...], lf2[r0:r0 + tw], preferred_element_type=F32)
        parts.append(cs2[:, :LANES] + cs2[:, LANES:])
    b = jnp.concatenate(parts, axis=0)

    kf3 = kf.reshape(tr // 8, 8, LANES)
    b3 = b.reshape(tr // 8, 8, LANES)
    diags = [jnp.sum(qf * kf, axis=-1, keepdims=True)]
    for dlt in range(1, 8):
        kd = pltpu.roll(kf3, dlt, axis=1).reshape(tr, LANES)
        bd = pltpu.roll(b3, dlt, axis=1).reshape(tr, LANES)
        term = qf * kd * jnp.exp(jnp.minimum(b - bd, 0.0))
        diags.append(jnp.sum(term, axis=-1, keepdims=True))
    lv_q, lv_k = [], []
    for w in (8, 16, 32):
        bend = jnp.broadcast_to(b.reshape(tr // w, w, LANES)[:, w - 1:w, :],
                                (tr // w, w, LANES)).reshape(tr, LANES)
        right = (row // w) % 2 == 1
        pend = pltpu.roll(bend, w, axis=0)
        lv_q.append(jnp.where(right, qf * jnp.exp(jnp.minimum(b - pend, 0.0)), 0.0).astype(BF16))
        lv_k.append(jnp.where(right, 0.0, kf * jnp.exp(jnp.minimum(bend - b, 0.0))).astype(BF16))

    blast = jnp.broadcast_to(b.reshape(nch, c, LANES)[:, c - 1:c, :], (nch, c, LANES)).reshape(tr, LANES)
    qdec = (qf * jnp.exp(b)).astype(BF16)
    kdec = (kf * jnp.exp(blast - b)).astype(BF16)
    dec = jnp.exp(blast)

    ri = lax.broadcasted_iota(jnp.int32, (c, c), 0)
    ci = lax.broadcasted_iota(jnp.int32, (c, c), 1)
    nt = (((1,), (1,)), ((), ()))
    sls = [slice(ch * c, (ch + 1) * c) for ch in range(nch)]
    sws = [[lax.dot_general(lv_q[li][sl], lv_k[li][sl], nt, preferred_element_type=F32) for li in range(3)]
           for sl in sls]
    upds = [lax.dot_general(vb[sl], kdec[sl], (((0,), (0,)), ((), ())), preferred_element_type=F32)
            for sl in sls]
    sames = [ri // (2 * w) == ci // (2 * w) for w in (8, 16, 32)]
    dsel = [(ci == ri - dlt) & (ri % 8 >= dlt) for dlt in range(8)]
    a_bf = []
    for ch, sl in enumerate(sls):
        a = jnp.zeros((c, c), F32)
        for li in range(3):
            a = a + jnp.where(sames[li], sws[ch][li], 0.0)
        for dlt in range(8):
            a = a + jnp.where(dsel[dlt], diags[dlt][sl], 0.0)
        a_bf.append(a.astype(BF16))
    st = st_ref[...]
    sts = []
    for ch in range(nch):
        sts.append(st.astype(BF16))
        st = dec[ch * c:ch * c + 1, :] * st + upds[ch]
    st_ref[...] = st
    outs = [lax.dot_general(qdec[sl], sts[ch], nt, preferred_element_type=F32)
            + jnp.dot(a_bf[ch], vb[sl], preferred_element_type=F32) for ch, sl in enumerate(sls)]

    o = jnp.concatenate(outs, axis=0)
    ms = jnp.mean(o * o, axis=-1, keepdims=True)
    y = (o * lax.rsqrt(ms + EPS) * ng_ref[...]) * _silu(g_ref[0].astype(F32))
    o_ref[0] = y.astype(o_ref.dtype)


def _hgrn(proj, f32f, lb_raw, norm_g, layer):
    bsz, s, _ = proj.shape
    tr = min(T_REC, s)
    nl = lb_raw.shape[0]
    cq, ci, cg = 12, 20, 24
    kern = functools.partial(_hgrn_kernel, tr=tr, layer=layer)
    tw = min(REC_TRI, tr)
    r = jnp.arange(tw)
    tri = ((r[:, None] // REC_CHUNK == r[None, :] // REC_CHUNK) & (r[None, :] <= r[:, None])).astype(BF16)
    return pl.pallas_call(
        kern,
        grid=(bsz, REC_HEADS, s // tr),
        in_specs=[pl.BlockSpec((nl, HEAD_W), lambda b, h, i: (0, h)),
                  pl.BlockSpec((tw, tw), lambda b, h, i: (0, 0)),
                  pl.BlockSpec((1, tr, HEAD_W), lambda b, h, i: (b, i, cq + h)),
                  pl.BlockSpec((1, tr, HEAD_W), lambda b, h, i: (b, i, h)),
                  pl.BlockSpec((1, tr, HEAD_W), lambda b, h, i: (b, i, ci + h)),
                  pl.BlockSpec((1, tr, HEAD_W), lambda b, h, i: (b, i, cg + h)),
                  pl.BlockSpec((1, HEAD_W), lambda b, h, i: (0, h))],
        out_specs=pl.BlockSpec((1, tr, HEAD_W), lambda b, h, i: (b, i, h)),
        out_shape=jax.ShapeDtypeStruct((bsz, s, REC_HEADS * HEAD_W), BF16),
        scratch_shapes=[pltpu.VMEM((HEAD_W, HEAD_W), F32)],
        compiler_params=_cparams(("arbitrary", "arbitrary", "arbitrary")),
        name="hgrn",
    )(lb_raw, tri, proj, f32f, proj, proj, norm_g.reshape(1, -1))


def _pack_bf16_pairs(x):
    n = x.shape[1] // 2
    lo = pltpu.bitcast(x[:, :n].astype(BF16).astype(F32), jnp.uint32)
    hi = pltpu.bitcast(x[:, n:].astype(BF16).astype(F32), jnp.uint32)
    return (hi & jnp.uint32(0xFFFF0000)) | (lo >> 16)


def _join_bf16_pairs(lo, hi):
    return pltpu.bitcast(hi, jnp.uint32) | (pltpu.bitcast(lo, jnp.uint32) >> 16)


def _unpack_bf16_pairs(w):
    lo = pltpu.bitcast(w << 16, F32)
    hi = pltpu.bitcast(w & jnp.uint32(0xFFFF0000), F32)
    return lo, hi


def _outproj_kernel(ya_ref, yr_ref, x_ref, mod_ref, g_ref, wo_ref, wr_ref, br_ref,
                    x1_ref, h2_ref, meta_ref, cnt_ref, run_ref, *, tm):
    first = (pl.program_id(0) == 0) & (pl.program_id(1) == 0)

    @pl.when(first)
    def _():
        run_ref[...] = jnp.zeros(run_ref.shape, F32)

    half = ya_ref.shape[2]
    mix = (jnp.dot(ya_ref[0], wo_ref[0:half, :], preferred_element_type=F32)
           + jnp.dot(yr_ref[0], wo_ref[half:2 * half, :], preferred_element_type=F32))
    x1 = x_ref[0] + mod_ref[0, 2:3, :] * mix
    x1_ref[0] = x1
    h2 = _norm_mod(x1, g_ref[...], mod_ref[0, 4:5, :], mod_ref[0, 3:4, :])
    h2_ref[0] = _pack_bf16_pairs(h2)

    logits = jnp.dot(h2.astype(BF16), wr_ref[...], preferred_element_type=F32) + br_ref[...]
    lane = lax.broadcasted_iota(jnp.int32, logits.shape, 1)
    lane_f = lane.astype(F32)

    def first_lane(mask):
        return jnp.min(jnp.where(mask, lane_f, float(LANES)), axis=-1, keepdims=True).astype(jnp.int32)

    is_g = lane < N_GROUPS
    gl = jnp.where(is_g, logits, NEG)
    gmax = jnp.max(gl, axis=-1, keepdims=True)
    gsum = jnp.sum(jnp.exp(gl - gmax), axis=-1, keepdims=True)
    g_gate = 1.0 / gsum
    g_idx = first_lane(is_g & (gl == gmax))
    e_lo = N_GROUPS + g_idx * EXPERTS_PER_GROUP
    in_grp = (lane >= e_lo) & (lane < e_lo + EXPERTS_PER_GROUP)
    el = jnp.where(in_grp, logits, NEG)
    m1 = jnp.max(el, axis=-1, keepdims=True)
    i1 = first_lane(in_grp & (el == m1))
    el2 = jnp.where(lane == i1, NEG, el)
    m2 = jnp.max(el2, axis=-1, keepdims=True)
    i2 = first_lane(in_grp & (el2 == m2))
    r = jnp.exp(m2 - m1)
    w1 = g_gate / (1.0 + r)
    w2 = g_gate * r / (1.0 + r)
    e1 = i1 - N_GROUPS
    e2 = i2 - N_GROUPS

    oh1 = (lane == e1).astype(F32)
    oh2 = (lane == e2).astype(F32)
    oh = oh1 + oh2
    ri = lax.broadcasted_iota(jnp.int32, (tm, tm), 0)
    ci = lax.broadcasted_iota(jnp.int32, (tm, tm), 1)
    tri = (ci < ri).astype(BF16)
    pre = jnp.dot(tri, oh.astype(BF16), preferred_element_type=F32) + run_ref[0:1, :]
    rk1 = jnp.sum(pre * oh1, axis=-1, keepdims=True)
    rk2 = jnp.sum(pre * oh2, axis=-1, keepdims=True)
    vals = (e1.astype(F32), e2.astype(F32), w1, w2, rk1, rk2)
    meta = jnp.zeros(logits.shape, F32)
    for i, v in enumerate(vals):
        meta = jnp.where(lane == i, v, meta)
    meta_ref[0] = meta
    run_ref[0:1, :] = run_ref[0:1, :] + jnp.sum(oh, axis=0, keepdims=True)
    cnt_ref[...] = run_ref[...]


def _outproj(ya, yr, x, mod6, g2n, wo_bf, wr_bf, br):
    bsz, s, d = x.shape
    tm = min(TM_PROJ, s)
    half = ya.shape[2]
    kern = functools.partial(_outproj_kernel, tm=tm)
    tok = lambda b, i: (b, i, 0)
    return pl.pallas_call(
        kern,
        grid=(bsz, s // tm),
        in_specs=[pl.BlockSpec((1, tm, half), tok),
                  pl.BlockSpec((1, tm, half), tok),
                  pl.BlockSpec((1, tm, d), tok),
                  pl.BlockSpec((1, 6, d), lambda b, i: (b, 0, 0)),
                  pl.BlockSpec((1, d), lambda b, i: (0, 0)),
                  pl.BlockSpec((2 * half, d), lambda b, i: (0, 0)),
                  pl.BlockSpec((d, LANES), lambda b, i: (0, 0)),
                  pl.BlockSpec((1, LANES), lambda b, i: (0, 0))],
        out_specs=[pl.BlockSpec((1, tm, d), tok),
                   pl.BlockSpec((1, tm, d // 2), tok),
                   pl.BlockSpec((1, tm, LANES), tok),
                   pl.BlockSpec((8, LANES), lambda b, i: (0, 0))],
        out_shape=[jax.ShapeDtypeStruct((bsz, s, d), F32),
                   jax.ShapeDtypeStruct((bsz, s, d // 2), jnp.uint32),
                   jax.ShapeDtypeStruct((bsz, s, LANES), F32),
                   jax.ShapeDtypeStruct((8, LANES), F32)],
        scratch_shapes=[pltpu.VMEM((8, LANES), F32)],
        compiler_params=_cparams(("arbitrary", "arbitrary")),
        name="outproj",
    )(ya, yr, x, mod6, g2n.reshape(1, d), wo_bf, wr_bf, br)


ROW_UNROLL = 8


def _dispatch_kernel(dest_ref, h_ref, rows_ref, sem, *, tm):
    def start(n, carry):
        for kk in range(TOP_K):
            d = dest_ref[n * TOP_K + kk]
            pltpu.make_async_copy(h_ref.at[pl.ds(n, 1)], rows_ref.at[pl.ds(d, 1)], sem).start(priority=kk)
        return carry

    lax.fori_loop(0, tm, start, 0, unroll=ROW_UNROLL)
    for kk in range(TOP_K):
        pltpu.make_async_copy(h_ref, rows_ref.at[pl.ds(0, tm)], sem).wait()


def _dispatch(dest, h2p):
    n, dw = h2p.shape
    tm = min(TM_PROJ, n)
    kern = functools.partial(_dispatch_kernel, tm=tm)
    return pl.pallas_call(
        kern,
        grid=(n // tm,),
        in_specs=[pl.BlockSpec((tm * TOP_K,), lambda i: (i,), memory_space=pltpu.SMEM),
                  pl.BlockSpec((tm, dw), lambda i: (i, 0))],
        out_specs=pl.BlockSpec(memory_space=pl.ANY),
        scratch_shapes=[pltpu.SemaphoreType.DMA(())],
        out_shape=jax.ShapeDtypeStruct((n * TOP_K, dw), jnp.uint32),
        compiler_params=_cparams(("arbitrary",)),
        name="dispatch",
    )(dest, h2p)


def _expert_kernel(wblk_ref, wexp_ref, wlo_ref, whi_ref, rows_ref, w1_ref, w3_ref, w2_ref, y_ref,
                   w1b_ref, w3b_ref, w2b_ref, acc_ref, *, blk, n_work):
    w = pl.program_id(0)
    prev = jnp.maximum(w - 1, 0)
    nxt = jnp.minimum(w + 1, n_work - 1)
    new_exp = (w == 0) | (wexp_ref[w] != wexp_ref[prev])
    new_blk = (w == 0) | (wblk_ref[w] != wblk_ref[prev])
    last_of_blk = (w == n_work - 1) | (wblk_ref[w] != wblk_ref[nxt])
    lo = wlo_ref[w]
    hi = whi_ref[w]

    @pl.when(new_exp)
    def _():
        w1b_ref[...] = w1_ref[0].astype(BF16)
        w3b_ref[...] = w3_ref[0].astype(BF16)
        w2b_ref[...] = w2_ref[0].astype(BF16)

    full = (lo == 0) & (hi == blk)

    @pl.when(new_blk & jnp.logical_not(full))
    def _():
        acc_ref[...] = jnp.zeros(acc_ref.shape, F32)

    def ffn():
        xlo, xhi = _unpack_bf16_pairs(rows_ref[...])
        half = xlo.shape[1]
        xlo = xlo.astype(BF16)
        xhi = xhi.astype(BF16)
        a = (jnp.dot(xlo, w1b_ref[0:half, :], preferred_element_type=F32)
             + jnp.dot(xhi, w1b_ref[half:2 * half, :], preferred_element_type=F32))
        b = (jnp.dot(xlo, w3b_ref[0:half, :], preferred_element_type=F32)
             + jnp.dot(xhi, w3b_ref[half:2 * half, :], preferred_element_type=F32))
        hmid = (_silu(a) * b).astype(BF16)
        return jnp.dot(hmid, w2b_ref[...], preferred_element_type=F32)

    @pl.when(full)
    def _():
        acc_ref[...] = ffn()

    @pl.when((hi > lo) & jnp.logical_not(full))
    def _():
        row = lax.broadcasted_iota(jnp.int32, (blk, 1), 0)
        keep = (row >= lo) & (row < hi)
        acc_ref[...] = acc_ref[...] + jnp.where(keep, ffn(), 0.0)

    @pl.when(last_of_blk)
    def _():
        y_ref[...] = _pack_bf16_pairs(acc_ref[...])


def _experts(wblk, wexp, wlo, whi, rows, w1, w3, w2):
    a_rows, dw = rows.shape
    n_work = wblk.shape[0]
    blk = min(BLK_E, a_rows)
    _, d, ff = w1.shape
    kern = functools.partial(_expert_kernel, blk=blk, n_work=n_work)
    return pl.pallas_call(
        kern,
        grid_spec=pltpu.PrefetchScalarGridSpec(
            num_scalar_prefetch=4,
            grid=(n_work,),
            in_specs=[pl.BlockSpec((blk, dw), lambda w, wb, we, wl, wh: (wb[w], 0)),
                      pl.BlockSpec((1, d, ff), lambda w, wb, we, wl, wh: (we[w], 0, 0)),
                      pl.BlockSpec((1, d, ff), lambda w, wb, we, wl, wh: (we[w], 0, 0)),
                      pl.BlockSpec((1, ff, d), lambda w, wb, we, wl, wh: (we[w], 0, 0))],
            out_specs=pl.BlockSpec((blk, dw), lambda w, wb, we, wl, wh: (wb[w], 0)),
            scratch_shapes=[pltpu.VMEM((d, ff), BF16),
                            pltpu.VMEM((d, ff), BF16),
                            pltpu.VMEM((ff, d), BF16),
                            pltpu.VMEM((blk, d), F32)]),
        out_shape=jax.ShapeDtypeStruct((a_rows, dw), jnp.uint32),
        compiler_params=_cparams(("arbitrary",)),
        name="experts",
    )(wblk, wexp, wlo, whi, rows, w1, w3, w2)


def _work_items(counts, a_rows, blk):
    n_blk = a_rows // blk
    ends = jnp.cumsum(counts)
    starts = ends - counts
    pts = jnp.sort(jnp.concatenate([jnp.arange(n_blk, dtype=jnp.int32) * blk, starts[1:]]))
    nxt = jnp.concatenate([pts[1:], jnp.array([a_rows], jnp.int32)])
    wblk = jnp.minimum(pts // blk, n_blk - 1)
    wexp = jnp.minimum(jnp.sum(ends[None, :] <= pts[:, None], axis=1), N_EXPERTS - 1).astype(jnp.int32)
    lo = pts - wblk * blk
    hi = jnp.minimum(nxt, (wblk + 1) * blk) - wblk * blk
    hi = jnp.maximum(hi, lo)
    return wblk.astype(jnp.int32), wexp, lo.astype(jnp.int32), hi.astype(jnp.int32)


def _combine_kernel(dest_ref, y_ref, x1_ref, meta_ref, mod_ref, g_ref, o_ref, buf_ref, sem, *, tm):
    def start(n, carry):
        for kk in range(TOP_K):
            d = dest_ref[n * TOP_K + kk]
            pltpu.make_async_copy(y_ref.at[pl.ds(d, 1)], buf_ref.at[kk, pl.ds(n, 1)], sem).start(priority=kk)
        return carry

    lax.fori_loop(0, tm, start, 0, unroll=ROW_UNROLL)
    for kk in range(TOP_K):
        pltpu.make_async_copy(y_ref.at[pl.ds(0, tm)], buf_ref.at[kk], sem).wait()

    meta = meta_ref[0]
    lo0, hi0 = _unpack_bf16_pairs(buf_ref[0])
    lo1, hi1 = _unpack_bf16_pairs(buf_ref[1])
    w0 = meta[:, 2:3]
    w1 = meta[:, 3:4]
    moe = jnp.concatenate([w0 * lo0 + w1 * lo1, w0 * hi0 + w1 * hi1], axis=1)
    x2 = x1_ref[0] + mod_ref[0, 5:6, :] * moe
    ms = jnp.mean(x2 * x2, axis=-1, keepdims=True)
    o_ref[0] = x2 * lax.rsqrt(ms + EPS) * g_ref[...]


def _combine(dest, ybuf, x1, meta, mod6, gf):
    bsz, s, d = x1.shape
    tm = min(TM_PROJ, s)
    kern = functools.partial(_combine_kernel, tm=tm)
    tok = lambda b, i: (b, i, 0)
    nt = s // tm
    return pl.pallas_call(
        kern,
        grid=(bsz, nt),
        in_specs=[pl.BlockSpec((tm * TOP_K,), lambda b, i: (b * nt + i,), memory_space=pltpu.SMEM),
                  pl.BlockSpec(memory_space=pl.ANY),
                  pl.BlockSpec((1, tm, d), tok),
                  pl.BlockSpec((1, tm, LANES), tok),
                  pl.BlockSpec((1, 6, d), lambda b, i: (b, 0, 0)),
                  pl.BlockSpec((1, d), lambda b, i: (0, 0))],
        out_specs=pl.BlockSpec((1, tm, d), tok),
        scratch_shapes=[pltpu.VMEM((TOP_K, tm, d // 2), jnp.uint32),
                        pltpu.SemaphoreType.DMA(())],
        out_shape=jax.ShapeDtypeStruct((bsz, s, d), F32),
        compiler_params=_cparams(("arbitrary", "arbitrary")),
        name="combine",
    )(dest, ybuf, x1, meta, mod6, gf.reshape(1, d))


def kernel(x, c, w_ada, b_ada, norm1_g, norm2_g, w_in, attn_lambda_q1, attn_lambda_k1, attn_lambda_q2,
           attn_lambda_k2, attn_subln_g, rel_bias_table, rec_lower_bound, rec_norm_g, w_out, w_group,
           b_group, w_expert, b_expert, w1, w3, w2, final_norm_g):
    bsz, s, d = x.shape
    depth = w_ada.shape[0]
    n_tok = bsz * s
    for l in range(depth):
        mod6 = _mod(c, w_ada[l], b_ada[l]).reshape(bsz, 6, d)
        proj, f32f = _inproj(x, mod6, norm1_g[l], w_in[l].astype(BF16))
        lq4 = jnp.stack([attn_lambda_q1[l], attn_lambda_k1[l], attn_lambda_q2[l], attn_lambda_k2[l]])
        ya = _attention(proj, lq4, attn_subln_g[l], rel_bias_table, l)
        yr = _hgrn(proj, f32f, rec_lower_bound, rec_norm_g[l], l)
        w_r = jnp.zeros((d, LANES), F32).at[:, :N_GROUPS].set(w_group[l])
        w_r = w_r.at[:, N_GROUPS:N_GROUPS + N_EXPERTS].set(w_expert[l]).astype(BF16)
        b_r = jnp.zeros((1, LANES), F32).at[0, :N_GROUPS].set(b_group[l])
        b_r = b_r.at[0, N_GROUPS:N_GROUPS + N_EXPERTS].set(b_expert[l])
        x1, h2p, meta, cnt = _outproj(ya, yr, x, mod6, norm2_g[l], w_out[l].astype(BF16), w_r, b_r)
        counts = cnt[0, :N_EXPERTS].astype(jnp.int32)
        starts = jnp.cumsum(counts) - counts
        eid = meta[..., 0:TOP_K].astype(jnp.int32)
        rank = meta[..., 4:4 + TOP_K].astype(jnp.int32)
        eoh = eid[..., None] == jnp.arange(N_EXPERTS, dtype=jnp.int32)
        dest = (jnp.sum(jnp.where(eoh, starts, 0), axis=-1) + rank).reshape(n_tok * TOP_K)
        rows = _dispatch(dest, h2p.reshape(n_tok, d // 2))
        wblk, wexp, wlo, whi = _work_items(counts, n_tok * TOP_K, min(BLK_E, n_tok * TOP_K))
        ybuf = _experts(wblk, wexp, wlo, whi, rows, w1[l], w3[l], w2[l])
        gf = final_norm_g if l == depth - 1 else jnp.ones((d,), F32)
        assert depth == 1
        x = _combine(dest, ybuf, x1, meta, mod6, gf)
    return x
```

```python
import functools
import math

import jax
import jax.numpy as jnp
from jax import lax
from jax.experimental import pallas as pl
from jax.experimental.pallas import tpu as pltpu

F32 = jnp.float32
BF16 = jnp.bfloat16
EPS = 1e-6

ATTN_HEADS = 4
ATTN_HEAD_DIM = 64
HEAD_W = 128
REC_HEADS = 4
REC_CHUNK = 64
NUM_BUCKETS = 32
MAX_DISTANCE = 128
N_GROUPS = 4
EXPERTS_PER_GROUP = 8
N_EXPERTS = 32
TOP_K = 2
NEG = -1e30

LANES = 128
VMEM_LIMIT = 56 * 1024 * 1024

TM_PROJ = 512
T_ATTN = 512
ATTN_COLS = 256
T_REC = 512
REC_TRI = 256
BLK_E = 512


def _cparams(sem):
    return pltpu.CompilerParams(dimension_semantics=sem, vmem_limit_bytes=VMEM_LIMIT)


def _sigmoid(x):
    return 0.5 * jnp.tanh(0.5 * x) + 0.5


def _silu(x):
    return x * _sigmoid(x)


def _mod_kernel(c_ref, w_ref, b_ref, o_ref):
    ca = _silu(c_ref[...])
    o_ref[...] = jnp.dot(ca, w_ref[...], preferred_element_type=F32,
                         precision=lax.Precision.HIGHEST) + b_ref[...]


def _mod(c, w, b):
    bsz, d = c.shape
    n = w.shape[1]
    tn = 1024
    return pl.pallas_call(
        _mod_kernel,
        grid=(n // tn,),
        in_specs=[pl.BlockSpec((bsz, d), lambda j: (0, 0)),
                  pl.BlockSpec((d, tn), lambda j: (0, j)),
                  pl.BlockSpec((1, tn), lambda j: (0, j))],
        out_specs=pl.BlockSpec((bsz, tn), lambda j: (0, j)),
        out_shape=jax.ShapeDtypeStruct((bsz, n), F32),
        compiler_params=_cparams(("arbitrary",)),
        name="mod",
    )(c, w, b.reshape(1, n))


def _norm_mod(x, g, sc, sh):
    ms = jnp.mean(x * x, axis=-1, keepdims=True)
    return (x * lax.rsqrt(ms + EPS) * g) * (1.0 + sc) + sh


def _inproj_kernel(x_ref, mod_ref, g_ref, w_ref, proj_ref, f_ref, *, f_chunk, n_chunks, cw):
    h = _norm_mod(x_ref[0], g_ref[...], mod_ref[0, 1:2, :], mod_ref[0, 0:1, :]).astype(BF16)
    for c in range(n_chunks):
        r = jnp.dot(h, w_ref[:, c * cw:(c + 1) * cw], preferred_element_type=F32)
        proj_ref[0, :, c * cw:(c + 1) * cw] = r.astype(BF16)
        if c == f_chunk:
            f_ref[0] = r


def _inproj(x, mod6, g, w_bf):
    bsz, s, d = x.shape
    n = w_bf.shape[1]
    cw = 512
    tm = min(TM_PROJ, s)
    kern = functools.partial(_inproj_kernel, f_chunk=4, n_chunks=n // cw, cw=cw)
    return pl.pallas_call(
        kern,
        grid=(bsz, s // tm),
        in_specs=[pl.BlockSpec((1, tm, d), lambda b, i: (b, i, 0)),
                  pl.BlockSpec((1, 6, d), lambda b, i: (b, 0, 0)),
                  pl.BlockSpec((1, d), lambda b, i: (0, 0)),
                  pl.BlockSpec((d, n), lambda b, i: (0, 0))],
        out_specs=[pl.BlockSpec((1, tm, n), lambda b, i: (b, i, 0)),
                   pl.BlockSpec((1, tm, cw), lambda b, i: (b, i, 0))],
        out_shape=[jax.ShapeDtypeStruct((bsz, s, n), BF16),
                   jax.ShapeDtypeStruct((bsz, s, cw), F32)],
        compiler_params=_cparams(("arbitrary", "arbitrary")),
        name="inproj",
    )(x, mod6, g.reshape(1, d), w_bf)


LOG2E = math.log2(math.e)


def _attn_kernel(lq_ref, q_ref, k_ref, v_ref, d_ref, g_ref, o_ref,
                 qs_ref, vt_ref, s_ref, m_ref, l_ref, acc_ref, *, t, lam_init):
    qi = pl.program_id(2)
    nkt = vt_ref.shape[0]

    @pl.when(qi == 0)
    def _():
        for c in range(nkt):
            vt_ref[c] = v_ref[0, c * t:(c + 1) * t, :].T

    q = q_ref[0]
    lane = lax.broadcasted_iota(jnp.int32, q.shape, 1)
    qf = q.astype(F32) * (ATTN_HEAD_DIM ** -0.5 * LOG2E)
    zero = jnp.zeros_like(qf)
    qs_ref[0:t, :] = jnp.where(lane < ATTN_HEAD_DIM, qf, zero).astype(BF16)
    qs_ref[t:2 * t, :] = jnp.where(lane >= ATTN_HEAD_DIM, qf, zero).astype(BF16)
    m_ref[...] = jnp.full(m_ref.shape, NEG, F32)
    l_ref[...] = jnp.zeros(l_ref.shape, F32)
    acc_ref[...] = jnp.zeros(acc_ref.shape, F32)

    blocks = [slice(c * ATTN_COLS, (c + 1) * ATTN_COLS) for c in range(2 * t // ATTN_COLS)]

    def scores(j, cs):
        k = k_ref[0, pl.ds(pl.multiple_of(j * t, t), t), :]
        return lax.dot_general(k, qs_ref[cs, :], (((1,), (1,)), ((), ())), preferred_element_type=F32)

    def tile(j, bias_ref, j_next):
        vt = vt_ref[j]
        m_all = m_ref[...]
        l_all = l_ref[...]
        ps, alphas, m_news, l_news = [], [], [], []
        for cs in blocks:
            s = s_ref[:, cs]
            if bias_ref is not None:
                s = s + bias_ref[:, cs]
            m_old = m_all[:, cs]
            m_new = jnp.maximum(m_old, jnp.max(s, axis=0, keepdims=True))
            alpha = jnp.exp2(m_old - m_new)
            p = jnp.exp2(s - m_new)
            l_news.append(alpha * l_all[:, cs] + jnp.sum(p, axis=0, keepdims=True))
            ps.append(p.astype(BF16))
            alphas.append(alpha)
            m_news.append(m_new)
            if j_next is not None:
                s_ref[:, cs] = scores(j_next, cs)
        pvs = [jnp.dot(vt, p, preferred_element_type=F32) for p in ps]
        for cs, alpha, pv in zip(blocks, alphas, pvs):
            acc_ref[:, cs] = alpha * acc_ref[:, cs] + pv
        m_ref[...] = jnp.concatenate(m_news, axis=1)
        l_ref[...] = jnp.concatenate(l_news, axis=1)

    for cs in blocks:
        s_ref[:, cs] = scores(0, cs)

    n_far = jnp.maximum(qi - 1, 0)

    def far_pair(jj, carry):
        tile(2 * jj, None, 2 * jj + 1)
        tile(2 * jj + 1, None, 2 * jj + 2)
        return carry

    lax.fori_loop(0, n_far // 2, far_pair, 0)

    @pl.when(n_far % 2 == 1)
    def _():
        tile(n_far - 1, None, n_far)

    @pl.when(qi >= 1)
    def _():
        tile(qi - 1, d_ref.at[0, 1], qi)

    tile(qi, d_ref.at[0, 0], None)

    lq = lq_ref[...]
    lam = (jnp.exp(jnp.sum(lq[0:1] * lq[1:2], axis=-1, keepdims=True))
           - jnp.exp(jnp.sum(lq[2:3] * lq[3:4], axis=-1, keepdims=True)) + lam_init)
    o = acc_ref[...] / l_ref[...]
    a = o[:, 0:t] - lam * o[:, t:2 * t]
    ms = jnp.mean(a * a, axis=0, keepdims=True)
    y = (a * lax.rsqrt(ms + EPS)).T * (g_ref[...] * (1.0 - lam_init))
    o_ref[0] = y.astype(o_ref.dtype)


def _t5_bucket(rel):
    n = jnp.maximum(rel, 0)
    max_exact = NUM_BUCKETS // 2
    nf = jnp.maximum(n, 1).astype(F32)
    large = max_exact + (jnp.log(nf / max_exact) / math.log(MAX_DISTANCE / max_exact)
                         * (NUM_BUCKETS - max_exact)).astype(jnp.int32)
    large = jnp.minimum(large, NUM_BUCKETS - 1)
    return jnp.where(n < max_exact, n, large)


def _bias_tiles(rel_table, t):
    assert t >= MAX_DISTANCE
    nh = rel_table.shape[1]
    rel = jnp.arange(2 * t)
    vec = (rel_table[_t5_bucket(rel)].astype(F32) - rel_table[NUM_BUCKETS - 1].astype(F32)[None, :]).T * LOG2E
    neg = jnp.full((nh, t), NEG, F32)

    def toeplitz(w):
        flat = jnp.tile(w, (1, t))[:, :t * (2 * t - 1)]
        return flat.reshape(nh, t, 2 * t - 1)[:, :, :t]

    diag = toeplitz(jnp.concatenate([vec[:, :t], neg], axis=1))
    prev = toeplitz(jnp.concatenate([vec[:, t:], vec[:, :t]], axis=1))
    both = jnp.stack([diag, prev], axis=1)
    return jnp.concatenate([both, both], axis=-1)


def _attention(proj, lq4, subln_g, rel_table, layer):
    bsz, s, _ = proj.shape
    t = min(T_ATTN, s)
    nq = s // t
    lam_init = 0.8 - 0.6 * math.exp(-0.3 * layer)
    dt = _bias_tiles(rel_table, t)
    hq, hk, hv = 0, ATTN_HEADS, 2 * ATTN_HEADS
    kern = functools.partial(_attn_kernel, t=t, lam_init=lam_init)
    return pl.pallas_call(
        kern,
        grid=(bsz, ATTN_HEADS, nq),
        in_specs=[pl.BlockSpec((4, ATTN_HEAD_DIM), lambda b, h, i: (0, 0)),
                  pl.BlockSpec((1, t, HEAD_W), lambda b, h, i: (b, i, hq + h)),
                  pl.BlockSpec((1, s, HEAD_W), lambda b, h, i: (b, 0, hk + h)),
                  pl.BlockSpec((1, s, HEAD_W), lambda b, h, i: (b, 0, hv + h)),
                  pl.BlockSpec((1, 2, t, 2 * t), lambda b, h, i: (h, 0, 0, 0)),
                  pl.BlockSpec((1, HEAD_W), lambda b, h, i: (0, 0))],
        out_specs=pl.BlockSpec((1, t, HEAD_W), lambda b, h, i: (b, i, h)),
        out_shape=jax.ShapeDtypeStruct((bsz, s, ATTN_HEADS * HEAD_W), BF16),
        scratch_shapes=[pltpu.VMEM((2 * t, HEAD_W), BF16),
                        pltpu.VMEM((nq, HEAD_W, t), BF16),
                        pltpu.VMEM((t, 2 * t), F32),
                        pltpu.VMEM((1, 2 * t), F32),
                        pltpu.VMEM((1, 2 * t), F32),
                        pltpu.VMEM((HEAD_W, 2 * t), F32)],
        compiler_params=_cparams(("arbitrary", "arbitrary", "arbitrary")),
        name="attn",
    )(lq4, proj, proj, proj, dt, subln_g.reshape(1, HEAD_W))


def _hgrn_kernel(lb_ref, tri_ref, q_ref, f_ref, i_ref, g_ref, ng_ref, o_ref, st_ref, *, tr, layer):
    c = REC_CHUNK
    nch = tr // c

    @pl.when(pl.program_id(2) == 0)
    def _():
        st_ref[...] = jnp.zeros(st_ref.shape, F32)

    lbr = lb_ref[...]
    e = jnp.exp(lbr - jnp.max(lbr, axis=0, keepdims=True))
    lb = jnp.sum(e[0:layer + 1], axis=0, keepdims=True) / jnp.sum(e, axis=0, keepdims=True)

    fr = f_ref[0]
    sg = _sigmoid(fr)
    lf = jnp.log(lb + (1.0 - lb) * sg)
    kf = (1.0 - lb) * (1.0 - sg)
    qf = _silu(q_ref[0].astype(F32))
    vb = i_ref[0]

    row = lax.broadcasted_iota(jnp.int32, (tr, LANES), 0)

    lf_hi = lf.astype(BF16)
    lf_lo = (lf - lf_hi.astype(F32)).astype(BF16)
    lf2 = jnp.concatenate([lf_hi, lf_lo], axis=1)
    tw = tri_ref.shape[0]
    parts = []
    for r0 in range(0, tr, tw):
        cs2 = jnp.dot(tri_ref[...], lf2[r0:r0 + tw], preferred_element_type=F32)
        parts.append(cs2[:, :LANES] + cs2[:, LANES:])
    b = jnp.concatenate(parts, axis=0)

    kf3 = kf.reshape(tr // 8, 8, LANES)
    b3 = b.reshape(tr // 8, 8, LANES)
    diags = [jnp.sum(qf * kf, axis=-1, keepdims=True)]
    for dlt in range(1, 8):
        kd = pltpu.roll(kf3, dlt, axis=1).reshape(tr, LANES)
        bd = pltpu.roll(b3, dlt, axis=1).reshape(tr, LANES)
        term = qf * kd * jnp.exp(jnp.minimum(b - bd, 0.0))
        diags.append(jnp.sum(term, axis=-1, keepdims=True))
    lv_q, lv_k = [], []
    for w in (8, 16, 32):
        bend = jnp.broadcast_to(b.reshape(tr // w, w, LANES)[:, w - 1:w, :],
                                (tr // w, w, LANES)).reshape(tr, LANES)
        right = (row // w) % 2 == 1
        pend = pltpu.roll(bend, w, axis=0)
        lv_q.append(jnp.where(right, qf * jnp.exp(jnp.minimum(b - pend, 0.0)), 0.0).astype(BF16))
        lv_k.append(jnp.where(right, 0.0, kf * jnp.exp(jnp.minimum(bend - b, 0.0))).astype(BF16))

    blast = jnp.broadcast_to(b.reshape(nch, c, LANES)[:, c - 1:c, :], (nch, c, LANES)).reshape(tr, LANES)
    qdec = (qf * jnp.exp(b)).astype(BF16)
    kdec = (kf * jnp.exp(blast - b)).astype(BF16)
    dec = jnp.exp(blast)

    ri = lax.broadcasted_iota(jnp.int32, (c, c), 0)
    ci = lax.broadcasted_iota(jnp.int32, (c, c), 1)
    nt = (((1,), (1,)), ((), ()))
    sls = [slice(ch * c, (ch + 1) * c) for ch in range(nch)]
    sws = [[lax.dot_general(lv_q[li][sl], lv_k[li][sl], nt, preferred_element_type=F32) for li in range(3)]
           for sl in sls]
    upds = [lax.dot_general(vb[sl], kdec[sl], (((0,), (0,)), ((), ())), preferred_element_type=F32)
            for sl in sls]
    sames = [ri // (2 * w) == ci // (2 * w) for w in (8, 16, 32)]
    dsel = [(ci == ri - dlt) & (ri % 8 >= dlt) for dlt in range(8)]
    a_bf = []
    for ch, sl in enumerate(sls):
        a = jnp.zeros((c, c), F32)
        for li in range(3):
            a = a + jnp.where(sames[li], sws[ch][li], 0.0)
        for dlt in range(8):
            a = a + jnp.where(dsel[dlt], diags[dlt][sl], 0.0)
        a_bf.append(a.astype(BF16))
    st = st_ref[...]
    sts = []
    for ch in range(nch):
        sts.append(st.astype(BF16))
        st = dec[ch * c:ch * c + 1, :] * st + upds[ch]
    st_ref[...] = st
    outs = [lax.dot_general(qdec[sl], sts[ch], nt, preferred_element_type=F32)
            + jnp.dot(a_bf[ch], vb[sl], preferred_element_type=F32) for ch, sl in enumerate(sls)]

    o = jnp.concatenate(outs, axis=0)
    ms = jnp.mean(o * o, axis=-1, keepdims=True)
    y = (o * lax.rsqrt(ms + EPS) * ng_ref[...]) * _silu(g_ref[0].astype(F32))
    o_ref[0] = y.astype(o_ref.dtype)


def _hgrn(proj, f32f, lb_raw, norm_g, layer):
    bsz, s, _ = proj.shape
    tr = min(T_REC, s)
    nl = lb_raw.shape[0]
    cq, ci, cg = 12, 20, 24
    kern = functools.partial(_hgrn_kernel, tr=tr, layer=layer)
    tw = min(REC_TRI, tr)
    r = jnp.arange(tw)
    tri = ((r[:, None] // REC_CHUNK == r[None, :] // REC_CHUNK) & (r[None, :] <= r[:, None])).astype(BF16)
    return pl.pallas_call(
        kern,
        grid=(bsz, REC_HEADS, s // tr),
        in_specs=[pl.BlockSpec((nl, HEAD_W), lambda b, h, i: (0, h)),
                  pl.BlockSpec((tw, tw), lambda b, h, i: (0, 0)),
                  pl.BlockSpec((1, tr, HEAD_W), lambda b, h, i: (b, i, cq + h)),
                  pl.BlockSpec((1, tr, HEAD_W), lambda b, h, i: (b, i, h)),
                  pl.BlockSpec((1, tr, HEAD_W), lambda b, h, i: (b, i, ci + h)),
                  pl.BlockSpec((1, tr, HEAD_W), lambda b, h, i: (b, i, cg + h)),
                  pl.BlockSpec((1, HEAD_W), lambda b, h, i: (0, h))],
        out_specs=pl.BlockSpec((1, tr, HEAD_W), lambda b, h, i: (b, i, h)),
        out_shape=jax.ShapeDtypeStruct((bsz, s, REC_HEADS * HEAD_W), BF16),
        scratch_shapes=[pltpu.VMEM((HEAD_W, HEAD_W), F32)],
        compiler_params=_cparams(("arbitrary", "arbitrary", "arbitrary")),
        name="hgrn",
    )(lb_raw, tri, proj, f32f, proj, proj, norm_g.reshape(1, -1))


def _pack_bf16_pairs(x):
    n = x.shape[1] // 2
    lo = pltpu.bitcast(x[:, :n].astype(BF16).astype(F32), jnp.uint32)
    hi = pltpu.bitcast(x[:, n:].astype(BF16).astype(F32), jnp.uint32)
    return (hi & jnp.uint32(0xFFFF0000)) | (lo >> 16)


def _join_bf16_pairs(lo, hi):
    return pltpu.bitcast(hi, jnp.uint32) | (pltpu.bitcast(lo, jnp.uint32) >> 16)


def _unpack_bf16_pairs(w):
    lo = pltpu.bitcast(w << 16, F32)
    hi = pltpu.bitcast(w & jnp.uint32(0xFFFF0000), F32)
    return lo, hi


def _outproj_kernel(ya_ref, yr_ref, x_ref, mod_ref, g_ref, wo_ref, wr_ref, br_ref,
                    x1_ref, h2_ref, meta_ref, metat_ref, cnt_ref, *, tm):
    half = ya_ref.shape[2]
    mix = (jnp.dot(ya_ref[0], wo_ref[0:half, :], preferred_element_type=F32)
           + jnp.dot(yr_ref[0], wo_ref[half:2 * half, :], preferred_element_type=F32))
    x1 = x_ref[0] + mod_ref[0, 2:3, :] * mix
    x1_ref[0] = x1
    h2 = _norm_mod(x1, g_ref[...], mod_ref[0, 4:5, :], mod_ref[0, 3:4, :])
    h2_ref[0] = _pack_bf16_pairs(h2)

    logits = jnp.dot(h2.astype(BF16), wr_ref[...], preferred_element_type=F32) + br_ref[...]
    lane = lax.broadcasted_iota(jnp.int32, logits.shape, 1)
    lane_f = lane.astype(F32)

    def first_lane(mask):
        return jnp.min(jnp.where(mask, lane_f, float(LANES)), axis=-1, keepdims=True).astype(jnp.int32)

    is_g = lane < N_GROUPS
    gl = jnp.where(is_g, logits, NEG)
    gmax = jnp.max(gl, axis=-1, keepdims=True)
    gsum = jnp.sum(jnp.exp(gl - gmax), axis=-1, keepdims=True)
    g_gate = 1.0 / gsum
    g_idx = first_lane(is_g & (gl == gmax))
    e_lo = N_GROUPS + g_idx * EXPERTS_PER_GROUP
    in_grp = (lane >= e_lo) & (lane < e_lo + EXPERTS_PER_GROUP)
    el = jnp.where(in_grp, logits, NEG)
    m1 = jnp.max(el, axis=-1, keepdims=True)
    i1 = first_lane(in_grp & (el == m1))
    el2 = jnp.where(lane == i1, NEG, el)
    m2 = jnp.max(el2, axis=-1, keepdims=True)
    i2 = first_lane(in_grp & (el2 == m2))
    r = jnp.exp(m2 - m1)
    w1 = g_gate / (1.0 + r)
    w2 = g_gate * r / (1.0 + r)
    e1 = i1 - N_GROUPS
    e2 = i2 - N_GROUPS

    oh1 = (lane == e1).astype(F32)
    oh2 = (lane == e2).astype(F32)
    oh = oh1 + oh2
    ri = lax.broadcasted_iota(jnp.int32, (tm, tm), 0)
    ci = lax.broadcasted_iota(jnp.int32, (tm, tm), 1)
    tri = (ci < ri).astype(BF16)
    rank = jnp.dot(tri, oh.astype(BF16), preferred_element_type=F32)
    cnt = jnp.sum(oh, axis=0, keepdims=True)
    cnt_up = jnp.floor((cnt + (GROUP - 1.0)) * (1.0 / GROUP)) * GROUP
    li = lax.broadcasted_iota(jnp.int32, (LANES, LANES), 0)
    lj = lax.broadcasted_iota(jnp.int32, (LANES, LANES), 1)
    before = (li < lj).astype(BF16)
    lstart = jnp.dot(jnp.broadcast_to(cnt_up, (8, LANES)).astype(BF16), before,
                     preferred_element_type=F32)[0:1, :]
    pos = rank + lstart
    p1 = jnp.sum(pos * oh1, axis=-1, keepdims=True)
    p2 = jnp.sum(pos * oh2, axis=-1, keepdims=True)
    vals = (p1, p2, w1, w2)
    meta = jnp.zeros(logits.shape, F32)
    for i, v in enumerate(vals):
        meta = jnp.where(lane == i, v, meta)
    meta_ref[0] = meta
    metat_ref[0] = meta.T[0:8, :]
    cnt_ref[0] = jnp.broadcast_to(cnt, (8, LANES))


def _outproj(ya, yr, x, mod6, g2n, wo_bf, wr_bf, br):
    bsz, s, d = x.shape
    tm = min(TM_PROJ, s)
    half = ya.shape[2]
    kern = functools.partial(_outproj_kernel, tm=tm)
    tok = lambda b, i: (b, i, 0)
    nt = s // tm
    tile = lambda b, i: (b * nt + i, 0, 0)
    return pl.pallas_call(
        kern,
        grid=(bsz, nt),
        in_specs=[pl.BlockSpec((1, tm, half), tok),
                  pl.BlockSpec((1, tm, half), tok),
                  pl.BlockSpec((1, tm, d), tok),
                  pl.BlockSpec((1, 6, d), lambda b, i: (b, 0, 0)),
                  pl.BlockSpec((1, d), lambda b, i: (0, 0)),
                  pl.BlockSpec((2 * half, d), lambda b, i: (0, 0)),
                  pl.BlockSpec((d, LANES), lambda b, i: (0, 0)),
                  pl.BlockSpec((1, LANES), lambda b, i: (0, 0))],
        out_specs=[pl.BlockSpec((1, tm, d), tok),
                   pl.BlockSpec((1, tm, d // 2), tok),
                   pl.BlockSpec((1, tm, LANES), tok),
                   pl.BlockSpec((1, 8, tm), tile),
                   pl.BlockSpec((1, 8, LANES), tile)],
        out_shape=[jax.ShapeDtypeStruct((bsz, s, d), F32),
                   jax.ShapeDtypeStruct((bsz, s, d // 2), jnp.uint32),
                   jax.ShapeDtypeStruct((bsz, s, LANES), F32),
                   jax.ShapeDtypeStruct((bsz * nt, 8, tm), F32),
                   jax.ShapeDtypeStruct((bsz * nt, 8, LANES), F32)],
        compiler_params=_cparams(("arbitrary", "arbitrary")),
        name="outproj",
    )(ya, yr, x, mod6, g2n.reshape(1, d), wo_bf, wr_bf, br)


GROUP = 8
DMA_UNROLL = 8


def _local_groups(tm):
    need = (tm * TOP_K + N_EXPERTS * (GROUP - 1)) // GROUP
    return -(-need // DMA_UNROLL) * DMA_UNROLL


def _dispatch_kernel(gdst_ref, metat_ref, h_ref, rows_ref, sort_ref, sem, *, tm, ng):
    tile = pl.program_id(0)
    p1 = metat_ref[0, 0:1, :].astype(jnp.int32)
    p2 = metat_ref[0, 1:2, :].astype(jnp.int32)
    r = lax.broadcasted_iota(jnp.int32, (ng * GROUP, tm), 0)
    perm = ((r == p1) | (r == p2)).astype(BF16)
    lo, hi = _unpack_bf16_pairs(h_ref[...])
    slo = jnp.dot(perm, lo.astype(BF16), preferred_element_type=F32)
    shi = jnp.dot(perm, hi.astype(BF16), preferred_element_type=F32)
    sort_ref[...] = _join_bf16_pairs(slo, shi).reshape(sort_ref.shape)

    def start(g, carry):
        dst = gdst_ref[tile * ng + g]
        pltpu.make_async_copy(sort_ref.at[g], rows_ref.at[dst], sem).start()
        return carry

    lax.fori_loop(0, ng, start, 0, unroll=DMA_UNROLL)
    pltpu.make_async_copy(sort_ref, rows_ref.at[pl.ds(0, ng)], sem).wait()


def _dispatch(gdst, meta_t, h2p):
    n, dw = h2p.shape
    tm = meta_t.shape[2]
    ng = _local_groups(tm)
    kern = functools.partial(_dispatch_kernel, tm=tm, ng=ng)
    return pl.pallas_call(
        kern,
        grid_spec=pltpu.PrefetchScalarGridSpec(
            num_scalar_prefetch=1,
            grid=(n // tm,),
            in_specs=[pl.BlockSpec((1, 8, tm), lambda i, a: (i, 0, 0)),
                      pl.BlockSpec((tm, dw), lambda i, a: (i, 0))],
            out_specs=pl.BlockSpec(memory_space=pl.ANY),
            scratch_shapes=[pltpu.VMEM((ng, GROUP, dw), jnp.uint32),
                            pltpu.SemaphoreType.DMA(())]),
        out_shape=jax.ShapeDtypeStruct((n // tm * ng, GROUP, dw), jnp.uint32),
        compiler_params=_cparams(("arbitrary",)),
        name="dispatch",
    )(gdst, meta_t, h2p)


def _group_tables(tcnt, tm):
    ng = _local_groups(tm)
    g8 = (tcnt + (GROUP - 1)) // GROUP
    gcount = jnp.sum(g8, axis=0)
    gstart = jnp.cumsum(gcount) - gcount
    gtile = gstart[None, :] + jnp.cumsum(g8, axis=0) - g8
    lend = jnp.cumsum(g8, axis=1)
    lstart = lend - g8
    g = jnp.arange(ng, dtype=jnp.int32)
    exp_of = jnp.sum(lend[:, None, :] <= g[None, :, None], axis=-1)
    onehot = exp_of[..., None] == jnp.arange(N_EXPERTS, dtype=jnp.int32)
    base = jnp.sum(jnp.where(onehot, (gtile - lstart)[:, None, :], 0), axis=-1)
    n_used = lend[:, -1:]
    used = g[None, :] < n_used
    n_spare = ng - n_used
    spare0 = jnp.sum(gcount) + jnp.cumsum(n_spare, axis=0) - n_spare
    gdst = jnp.where(used, base + g[None, :], spare0 + g[None, :] - n_used).astype(jnp.int32).reshape(-1)
    gsrc = jnp.where(used, base + g[None, :], 0).astype(jnp.int32).reshape(-1)
    return gdst, gsrc, gcount * GROUP


def _expert_kernel(wblk_ref, wexp_ref, wlo_ref, whi_ref, rows_ref, w1_ref, w3_ref, w2_ref, y_ref,
                   w1b_ref, w3b_ref, w2b_ref, acc_ref, *, blk, n_work):
    w = pl.program_id(0)
    prev = jnp.maximum(w - 1, 0)
    nxt = jnp.minimum(w + 1, n_work - 1)
    new_exp = (w == 0) | (wexp_ref[w] != wexp_ref[prev])
    new_blk = (w == 0) | (wblk_ref[w] != wblk_ref[prev])
    last_of_blk = (w == n_work - 1) | (wblk_ref[w] != wblk_ref[nxt])
    lo = wlo_ref[w]
    hi = whi_ref[w]

    @pl.when(new_exp)
    def _():
        w1b_ref[...] = w1_ref[0].astype(BF16)
        w3b_ref[...] = w3_ref[0].astype(BF16)
        w2b_ref[...] = w2_ref[0].astype(BF16)

    full = (lo == 0) & (hi == blk)

    @pl.when(new_blk & jnp.logical_not(full))
    def _():
        acc_ref[...] = jnp.zeros(acc_ref.shape, F32)

    def ffn():
        xlo, xhi = _unpack_bf16_pairs(rows_ref[...])
        half = xlo.shape[1]
        xlo = xlo.astype(BF16)
        xhi = xhi.astype(BF16)
        a = (jnp.dot(xlo, w1b_ref[0:half, :], preferred_element_type=F32)
             + jnp.dot(xhi, w1b_ref[half:2 * half, :], preferred_element_type=F32))
        b = (jnp.dot(xlo, w3b_ref[0:half, :], preferred_element_type=F32)
             + jnp.dot(xhi, w3b_ref[half:2 * half, :], preferred_element_type=F32))
        hmid = (_silu(a) * b).astype(BF16)
        return jnp.dot(hmid, w2b_ref[...], preferred_element_type=F32)

    @pl.when(full)
    def _():
        acc_ref[...] = ffn()

    @pl.when((hi > lo) & jnp.logical_not(full))
    def _():
        row = lax.broadcasted_iota(jnp.int32, (blk, 1), 0)
        keep = (row >= lo) & (row < hi)
        acc_ref[...] = acc_ref[...] + jnp.where(keep, ffn(), 0.0)

    @pl.when(last_of_blk)
    def _():
        y_ref[...] = _pack_bf16_pairs(acc_ref[...])


def _experts(wblk, wexp, wlo, whi, rows, a_rows, w1, w3, w2):
    dw = rows.shape[1]
    n_work = wblk.shape[0]
    blk = math.gcd(BLK_E, a_rows)
    _, d, ff = w1.shape
    kern = functools.partial(_expert_kernel, blk=blk, n_work=n_work)
    return pl.pallas_call(
        kern,
        grid_spec=pltpu.PrefetchScalarGridSpec(
            num_scalar_prefetch=4,
            grid=(n_work,),
            in_specs=[pl.BlockSpec((blk, dw), lambda w, wb, we, wl, wh: (wb[w], 0)),
                      pl.BlockSpec((1, d, ff), lambda w, wb, we, wl, wh: (we[w], 0, 0)),
                      pl.BlockSpec((1, d, ff), lambda w, wb, we, wl, wh: (we[w], 0, 0)),
                      pl.BlockSpec((1, ff, d), lambda w, wb, we, wl, wh: (we[w], 0, 0))],
            out_specs=pl.BlockSpec((blk, dw), lambda w, wb, we, wl, wh: (wb[w], 0)),
            scratch_shapes=[pltpu.VMEM((d, ff), BF16),
                            pltpu.VMEM((d, ff), BF16),
                            pltpu.VMEM((ff, d), BF16),
                            pltpu.VMEM((blk, d), F32)]),
        out_shape=jax.ShapeDtypeStruct((a_rows, dw), jnp.uint32),
        compiler_params=_cparams(("arbitrary",)),
        name="experts",
    )(wblk, wexp, wlo, whi, rows, w1, w3, w2)


def _work_items(counts, a_rows, blk):
    n_blk = a_rows // blk
    ends = jnp.cumsum(counts)
    starts = ends - counts
    total = ends[-1]
    pts = jnp.sort(jnp.concatenate([jnp.arange(n_blk, dtype=jnp.int32) * blk, starts[1:]]))
    nxt = jnp.concatenate([pts[1:], jnp.array([a_rows], jnp.int32)])
    blk_of = jnp.minimum(pts // blk, n_blk - 1)
    wexp = jnp.minimum(jnp.sum(ends[None, :] <= pts[:, None], axis=1), N_EXPERTS - 1).astype(jnp.int32)
    lo = pts - blk_of * blk
    hi = jnp.minimum(jnp.minimum(nxt, total), (blk_of + 1) * blk) - blk_of * blk
    hi = jnp.maximum(hi, lo)
    return blk_of.astype(jnp.int32), wexp, lo.astype(jnp.int32), hi.astype(jnp.int32)


def _combine_kernel(gsrc_ref, y_ref, x1_ref, meta_ref, mod_ref, g_ref, o_ref, sort_ref, sem, *, tm, ng):
    tile = pl.program_id(0) * pl.num_programs(1) + pl.program_id(1)

    def start(g, carry):
        src = gsrc_ref[tile * ng + g]
        pltpu.make_async_copy(y_ref.at[src], sort_ref.at[g], sem).start()
        return carry

    lax.fori_loop(0, ng, start, 0, unroll=DMA_UNROLL)
    pltpu.make_async_copy(y_ref.at[pl.ds(0, ng)], sort_ref, sem).wait()

    meta = meta_ref[0]
    nrow = ng * GROUP
    lo, hi = _unpack_bf16_pairs(sort_ref[...].reshape(nrow, sort_ref.shape[2]))
    lo = lo.astype(BF16)
    hi = hi.astype(BF16)
    col = lax.broadcasted_iota(jnp.int32, (tm, nrow), 1)
    mix = jnp.zeros((tm, nrow), F32)
    for kk in range(TOP_K):
        mix = jnp.where(col == meta[:, kk:kk + 1].astype(jnp.int32), meta[:, TOP_K + kk:TOP_K + kk + 1], mix)
    mix = mix.astype(BF16)
    moe = jnp.concatenate([jnp.dot(mix, lo, preferred_element_type=F32),
                           jnp.dot(mix, hi, preferred_element_type=F32)], axis=1)
    x2 = x1_ref[0] + mod_ref[0, 5:6, :] * moe
    ms = jnp.mean(x2 * x2, axis=-1, keepdims=True)
    o_ref[0] = x2 * lax.rsqrt(ms + EPS) * g_ref[...]


def _combine(gsrc, ybuf3, x1, meta, mod6, gf):
    bsz, s, d = x1.shape
    tm = min(TM_PROJ, s)
    ng = _local_groups(tm)
    kern = functools.partial(_combine_kernel, tm=tm, ng=ng)
    tok = lambda b, i, a: (b, i, 0)
    return pl.pallas_call(
        kern,
        grid_spec=pltpu.PrefetchScalarGridSpec(
            num_scalar_prefetch=1,
            grid=(bsz, s // tm),
            in_specs=[pl.BlockSpec(memory_space=pl.ANY),
                      pl.BlockSpec((1, tm, d), tok),
                      pl.BlockSpec((1, tm, LANES), tok),
                      pl.BlockSpec((1, 6, d), lambda b, i, a: (b, 0, 0)),
                      pl.BlockSpec((1, d), lambda b, i, a: (0, 0))],
            out_specs=pl.BlockSpec((1, tm, d), tok),
            scratch_shapes=[pltpu.VMEM((ng, GROUP, d // 2), jnp.uint32),
                            pltpu.SemaphoreType.DMA(())]),
        out_shape=jax.ShapeDtypeStruct((bsz, s, d), F32),
        compiler_params=_cparams(("arbitrary", "arbitrary")),
        name="combine",
    )(gsrc, ybuf3, x1, meta, mod6, gf.reshape(1, d))


def kernel(x, c, w_ada, b_ada, norm1_g, norm2_g, w_in, attn_lambda_q1, attn_lambda_k1, attn_lambda_q2,
           attn_lambda_k2, attn_subln_g, rel_bias_table, rec_lower_bound, rec_norm_g, w_out, w_group,
           b_group, w_expert, b_expert, w1, w3, w2, final_norm_g):
    bsz, s, d = x.shape
    depth = w_ada.shape[0]
    n_tok = bsz * s
    for l in range(depth):
        mod6 = _mod(c, w_ada[l], b_ada[l]).reshape(bsz, 6, d)
        proj, f32f = _inproj(x, mod6, norm1_g[l], w_in[l].astype(BF16))
        lq4 = jnp.stack([attn_lambda_q1[l], attn_lambda_k1[l], attn_lambda_q2[l], attn_lambda_k2[l]])
        ya = _attention(proj, lq4, attn_subln_g[l], rel_bias_table, l)
        yr = _hgrn(proj, f32f, rec_lower_bound, rec_norm_g[l], l)
        w_r = jnp.zeros((d, LANES), F32).at[:, :N_GROUPS].set(w_group[l])
        w_r = w_r.at[:, N_GROUPS:N_GROUPS + N_EXPERTS].set(w_expert[l]).astype(BF16)
        b_r = jnp.zeros((1, LANES), F32).at[0, :N_GROUPS].set(b_group[l])
        b_r = b_r.at[0, N_GROUPS:N_GROUPS + N_EXPERTS].set(b_expert[l])
        x1, h2p, meta, meta_t, cnt = _outproj(ya, yr, x, mod6, norm2_g[l], w_out[l].astype(BF16), w_r, b_r)
        tm = meta_t.shape[2]
        tcnt = cnt[:, 0, :N_EXPERTS].astype(jnp.int32)
        gdst, gsrc, counts_up = _group_tables(tcnt, tm)
        rows3 = _dispatch(gdst, meta_t, h2p.reshape(n_tok, d // 2))
        buf_rows = rows3.shape[0] * GROUP
        blk = math.gcd(BLK_E, buf_rows)
        wblk, wexp, wlo, whi = _work_items(counts_up, buf_rows, blk)
        ybuf = _experts(wblk, wexp, wlo, whi, rows3.reshape(buf_rows, d // 2), buf_rows, w1[l], w3[l], w2[l])
        gf = final_norm_g if l == depth - 1 else jnp.ones((d,), F32)
        assert depth == 1
        x = _combine(gsrc, ybuf.reshape(buf_rows // GROUP, GROUP, d // 2), x1, meta, mod6, gf)
    return x
```

```python
import functools
import math

import jax
import jax.numpy as jnp
from jax import lax
from jax.experimental import pallas as pl
from jax.experimental.pallas import tpu as pltpu

F32 = jnp.float32
BF16 = jnp.bfloat16
EPS = 1e-6

ATTN_HEADS = 4
ATTN_HEAD_DIM = 64
HEAD_W = 128
REC_HEADS = 4
REC_CHUNK = 64
NUM_BUCKETS = 32
MAX_DISTANCE = 128
N_GROUPS = 4
EXPERTS_PER_GROUP = 8
N_EXPERTS = 32
TOP_K = 2
NEG = -1e30

LANES = 128
VMEM_LIMIT = 56 * 1024 * 1024

TM_PROJ = 512
T_ATTN = 512
ATTN_COLS = 256
T_REC = 512
REC_TRI = 256
BLK_E = 512


def _cparams(sem):
    return pltpu.CompilerParams(dimension_semantics=sem, vmem_limit_bytes=VMEM_LIMIT)


def _sigmoid(x):
    return 0.5 * jnp.tanh(0.5 * x) + 0.5


def _silu(x):
    return x * _sigmoid(x)


def _mod_kernel(c_ref, w_ref, b_ref, o_ref):
    ca = _silu(c_ref[...])
    o_ref[...] = jnp.dot(ca, w_ref[...], preferred_element_type=F32,
                         precision=lax.Precision.HIGHEST) + b_ref[...]


def _mod(c, w, b):
    bsz, d = c.shape
    n = w.shape[1]
    tn = 1024
    return pl.pallas_call(
        _mod_kernel,
        grid=(n // tn,),
        in_specs=[pl.BlockSpec((bsz, d), lambda j: (0, 0)),
                  pl.BlockSpec((d, tn), lambda j: (0, j)),
                  pl.BlockSpec((1, tn), lambda j: (0, j))],
        out_specs=pl.BlockSpec((bsz, tn), lambda j: (0, j)),
        out_shape=jax.ShapeDtypeStruct((bsz, n), F32),
        compiler_params=_cparams(("arbitrary",)),
        name="mod",
    )(c, w, b.reshape(1, n))


def _norm_mod(x, g, sc, sh):
    ms = jnp.mean(x * x, axis=-1, keepdims=True)
    return (x * lax.rsqrt(ms + EPS) * g) * (1.0 + sc) + sh


def _inproj_kernel(x_ref, mod_ref, g_ref, w_ref, proj_ref, f_ref, *, f_chunk, n_chunks, cw):
    h = _norm_mod(x_ref[0], g_ref[...], mod_ref[0, 1:2, :], mod_ref[0, 0:1, :]).astype(BF16)
    for c in range(n_chunks):
        r = jnp.dot(h, w_ref[:, c * cw:(c + 1) * cw], preferred_element_type=F32)
        proj_ref[0, :, c * cw:(c + 1) * cw] = r.astype(BF16)
        if c == f_chunk:
            f_ref[0] = r


def _inproj(x, mod6, g, w_bf):
    bsz, s, d = x.shape
    n = w_bf.shape[1]
    cw = 512
    tm = min(TM_PROJ, s)
    kern = functools.partial(_inproj_kernel, f_chunk=4, n_chunks=n // cw, cw=cw)
    return pl.pallas_call(
        kern,
        grid=(bsz, s // tm),
        in_specs=[pl.BlockSpec((1, tm, d), lambda b, i: (b, i, 0)),
                  pl.BlockSpec((1, 6, d), lambda b, i: (b, 0, 0)),
                  pl.BlockSpec((1, d), lambda b, i: (0, 0)),
                  pl.BlockSpec((d, n), lambda b, i: (0, 0))],
        out_specs=[pl.BlockSpec((1, tm, n), lambda b, i: (b, i, 0)),
                   pl.BlockSpec((1, tm, cw), lambda b, i: (b, i, 0))],
        out_shape=[jax.ShapeDtypeStruct((bsz, s, n), BF16),
                   jax.ShapeDtypeStruct((bsz, s, cw), F32)],
        compiler_params=_cparams(("arbitrary", "arbitrary")),
        name="inproj",
    )(x, mod6, g.reshape(1, d), w_bf)


LOG2E = math.log2(math.e)


def _attn_kernel(lq_ref, q_ref, k_ref, v_ref, d_ref, g_ref, o_ref,
                 qs_ref, vt_ref, s_ref, m_ref, l_ref, acc_ref, *, t, lam_init):
    qi = pl.program_id(2)
    nkt = vt_ref.shape[0]

    @pl.when(qi == 0)
    def _():
        for c in range(nkt):
            vt_ref[c] = v_ref[0, c * t:(c + 1) * t, :].T

    q = q_ref[0]
    lane = lax.broadcasted_iota(jnp.int32, q.shape, 1)
    qf = q.astype(F32) * (ATTN_HEAD_DIM ** -0.5 * LOG2E)
    zero = jnp.zeros_like(qf)
    qs_ref[0:t, :] = jnp.where(lane < ATTN_HEAD_DIM, qf, zero).astype(BF16)
    qs_ref[t:2 * t, :] = jnp.where(lane >= ATTN_HEAD_DIM, qf, zero).astype(BF16)
    m_ref[...] = jnp.full(m_ref.shape, NEG, F32)
    l_ref[...] = jnp.zeros(l_ref.shape, F32)
    acc_ref[...] = jnp.zeros(acc_ref.shape, F32)

    blocks = [slice(c * ATTN_COLS, (c + 1) * ATTN_COLS) for c in range(2 * t // ATTN_COLS)]

    def scores(j, cs):
        k = k_ref[0, pl.ds(pl.multiple_of(j * t, t), t), :]
        return lax.dot_general(k, qs_ref[cs, :], (((1,), (1,)), ((), ())), preferred_element_type=F32)

    def tile(j, bias_ref, j_next):
        vt = vt_ref[j]
        m_all = m_ref[...]
        l_all = l_ref[...]
        ps, alphas, m_news, l_news = [], [], [], []
        for cs in blocks:
            s = s_ref[:, cs]
            if bias_ref is not None:
                s = s + bias_ref[:, cs]
            m_old = m_all[:, cs]
            m_new = jnp.maximum(m_old, jnp.max(s, axis=0, keepdims=True))
            alpha = jnp.exp2(m_old - m_new)
            p = jnp.exp2(s - m_new)
            l_news.append(alpha * l_all[:, cs] + jnp.sum(p, axis=0, keepdims=True))
            ps.append(p.astype(BF16))
            alphas.append(alpha)
            m_news.append(m_new)
            if j_next is not None:
                s_ref[:, cs] = scores(j_next, cs)
        pvs = [jnp.dot(vt, p, preferred_element_type=F32) for p in ps]
        for cs, alpha, pv in zip(blocks, alphas, pvs):
            acc_ref[:, cs] = alpha * acc_ref[:, cs] + pv
        m_ref[...] = jnp.concatenate(m_news, axis=1)
        l_ref[...] = jnp.concatenate(l_news, axis=1)

    for cs in blocks:
        s_ref[:, cs] = scores(0, cs)

    n_far = jnp.maximum(qi - 1, 0)

    def far_pair(jj, carry):
        tile(2 * jj, None, 2 * jj + 1)
        tile(2 * jj + 1, None, 2 * jj + 2)
        return carry

    lax.fori_loop(0, n_far // 2, far_pair, 0)

    @pl.when(n_far % 2 == 1)
    def _():
        tile(n_far - 1, None, n_far)

    @pl.when(qi >= 1)
    def _():
        tile(qi - 1, d_ref.at[0, 1], qi)
        tile(qi, d_ref.at[0, 0], None)

    @pl.when(qi == 0)
    def _():
        tile(qi, d_ref.at[0, 0], None)

    lq = lq_ref[...]
    lam = (jnp.exp(jnp.sum(lq[0:1] * lq[1:2], axis=-1, keepdims=True))
           - jnp.exp(jnp.sum(lq[2:3] * lq[3:4], axis=-1, keepdims=True)) + lam_init)
    o = acc_ref[...] / l_ref[...]
    a = o[:, 0:t] - lam * o[:, t:2 * t]
    ms = jnp.mean(a * a, axis=0, keepdims=True)
    y = (a * lax.rsqrt(ms + EPS)).T * (g_ref[...] * (1.0 - lam_init))
    o_ref[0] = y.astype(o_ref.dtype)


def _t5_bucket(rel):
    n = jnp.maximum(rel, 0)
    max_exact = NUM_BUCKETS // 2
    nf = jnp.maximum(n, 1).astype(F32)
    large = max_exact + (jnp.log(nf / max_exact) / math.log(MAX_DISTANCE / max_exact)
                         * (NUM_BUCKETS - max_exact)).astype(jnp.int32)
    large = jnp.minimum(large, NUM_BUCKETS - 1)
    return jnp.where(n < max_exact, n, large)


def _bias_tiles(rel_table, t):
    assert t >= MAX_DISTANCE
    nh = rel_table.shape[1]
    rel = jnp.arange(2 * t)
    vec = (rel_table[_t5_bucket(rel)].astype(F32) - rel_table[NUM_BUCKETS - 1].astype(F32)[None, :]).T * LOG2E
    neg = jnp.full((nh, t), NEG, F32)

    def toeplitz(w):
        flat = jnp.tile(w, (1, t))[:, :t * (2 * t - 1)]
        return flat.reshape(nh, t, 2 * t - 1)[:, :, :t]

    diag = toeplitz(jnp.concatenate([vec[:, :t], neg], axis=1))
    prev = toeplitz(jnp.concatenate([vec[:, t:], vec[:, :t]], axis=1))
    both = jnp.stack([diag, prev], axis=1)
    return jnp.concatenate([both, both], axis=-1)


def _attention(proj, lq4, subln_g, rel_table, layer):
    bsz, s, _ = proj.shape
    t = min(T_ATTN, s)
    nq = s // t
    lam_init = 0.8 - 0.6 * math.exp(-0.3 * layer)
    dt = _bias_tiles(rel_table, t)
    hq, hk, hv = 0, ATTN_HEADS, 2 * ATTN_HEADS
    kern = functools.partial(_attn_kernel, t=t, lam_init=lam_init)
    return pl.pallas_call(
        kern,
        grid=(bsz, ATTN_HEADS, nq),
        in_specs=[pl.BlockSpec((4, ATTN_HEAD_DIM), lambda b, h, i: (0, 0)),
                  pl.BlockSpec((1, t, HEAD_W), lambda b, h, i: (b, i, hq + h)),
                  pl.BlockSpec((1, s, HEAD_W), lambda b, h, i: (b, 0, hk + h)),
                  pl.BlockSpec((1, s, HEAD_W), lambda b, h, i: (b, 0, hv + h)),
                  pl.BlockSpec((1, 2, t, 2 * t), lambda b, h, i: (h, 0, 0, 0)),
                  pl.BlockSpec((1, HEAD_W), lambda b, h, i: (0, 0))],
        out_specs=pl.BlockSpec((1, t, HEAD_W), lambda b, h, i: (b, i, h)),
        out_shape=jax.ShapeDtypeStruct((bsz, s, ATTN_HEADS * HEAD_W), BF16),
        scratch_shapes=[pltpu.VMEM((2 * t, HEAD_W), BF16),
                        pltpu.VMEM((nq, HEAD_W, t), BF16),
                        pltpu.VMEM((t, 2 * t), F32),
                        pltpu.VMEM((1, 2 * t), F32),
                        pltpu.VMEM((1, 2 * t), F32),
                        pltpu.VMEM((HEAD_W, 2 * t), F32)],
        compiler_params=_cparams(("arbitrary", "arbitrary", "arbitrary")),
        name="attn",
    )(lq4, proj, proj, proj, dt, subln_g.reshape(1, HEAD_W))


def _hgrn_kernel(lb_ref, tri_ref, q_ref, f_ref, i_ref, g_ref, ng_ref, o_ref, st_ref, *, tr, layer):
    c = REC_CHUNK
    nch = tr // c

    @pl.when(pl.program_id(2) == 0)
    def _():
        st_ref[...] = jnp.zeros(st_ref.shape, F32)

    lbr = lb_ref[...]
    e = jnp.exp(lbr - jnp.max(lbr, axis=0, keepdims=True))
    lb = jnp.sum(e[0:layer + 1], axis=0, keepdims=True) / jnp.sum(e, axis=0, keepdims=True)

    fr = f_ref[0]
    sg = _sigmoid(fr)
    lf = jnp.log(lb + (1.0 - lb) * sg)
    kf = (1.0 - lb) * (1.0 - sg)
    qf = _silu(q_ref[0].astype(F32))
    vb = i_ref[0]

    row = lax.broadcasted_iota(jnp.int32, (tr, LANES), 0)

    lf_hi = lf.astype(BF16)
    lf_lo = (lf - lf_hi.astype(F32)).astype(BF16)
    lf2 = jnp.concatenate([lf_hi, lf_lo], axis=1)
    tw = tri_ref.shape[0]
    parts = []
    for r0 in range(0, tr, tw):
        cs2 = jnp.dot(tri_ref[...], lf2[r0:r0 + tw], preferred_element_type=F32)
        parts.append(cs2[:, :LANES] + cs2[:, LANES:])
    b = jnp.concatenate(parts, axis=0)

    kf3 = kf.reshape(tr // 8, 8, LANES)
    b3 = b.reshape(tr // 8, 8, LANES)
    diags = [jnp.sum(qf * kf, axis=-1, keepdims=True)]
    for dlt in range(1, 8):
        kd = pltpu.roll(kf3, dlt, axis=1).reshape(tr, LANES)
        bd = pltpu.roll(b3, dlt, axis=1).reshape(tr, LANES)
        term = qf * kd * jnp.exp(jnp.minimum(b - bd, 0.0))
        diags.append(jnp.sum(term, axis=-1, keepdims=True))
    lv_q, lv_k = [], []
    for w in (8, 16, 32):
        bend = jnp.broadcast_to(b.reshape(tr // w, w, LANES)[:, w - 1:w, :],
                                (tr // w, w, LANES)).reshape(tr, LANES)
        right = (row // w) % 2 == 1
        pend = pltpu.roll(bend, w, axis=0)
        lv_q.append(jnp.where(right, qf * jnp.exp(jnp.minimum(b - pend, 0.0)), 0.0).astype(BF16))
        lv_k.append(jnp.where(right, 0.0, kf * jnp.exp(jnp.minimum(bend - b, 0.0))).astype(BF16))

    blast = jnp.broadcast_to(b.reshape(nch, c, LANES)[:, c - 1:c, :], (nch, c, LANES)).reshape(tr, LANES)
    qdec = (qf * jnp.exp(b)).astype(BF16)
    kdec = (kf * jnp.exp(blast - b)).astype(BF16)
    dec = jnp.exp(blast)

    ri = lax.broadcasted_iota(jnp.int32, (c, c), 0)
    ci = lax.broadcasted_iota(jnp.int32, (c, c), 1)
    nt = (((1,), (1,)), ((), ()))
    sls = [slice(ch * c, (ch + 1) * c) for ch in range(nch)]
    sws = [[lax.dot_general(lv_q[li][sl], lv_k[li][sl], nt, preferred_element_type=F32) for li in range(3)]
           for sl in sls]
    upds = [lax.dot_general(vb[sl], kdec[sl], (((0,), (0,)), ((), ())), preferred_element_type=F32)
            for sl in sls]
    sames = [ri // (2 * w) == ci // (2 * w) for w in (8, 16, 32)]
    dsel = [(ci == ri - dlt) & (ri % 8 >= dlt) for dlt in range(8)]
    a_bf = []
    for ch, sl in enumerate(sls):
        a = jnp.zeros((c, c), F32)
        for li in range(3):
            a = a + jnp.where(sames[li], sws[ch][li], 0.0)
        for dlt in range(8):
            a = a + jnp.where(dsel[dlt], diags[dlt][sl], 0.0)
        a_bf.append(a.astype(BF16))
    st = st_ref[...]
    sts = []
    for ch in range(nch):
        sts.append(st.astype(BF16))
        st = dec[ch * c:ch * c + 1, :] * st + upds[ch]
    st_ref[...] = st
    outs = [lax.dot_general(qdec[sl], sts[ch], nt, preferred_element_type=F32)
            + jnp.dot(a_bf[ch], vb[sl], preferred_element_type=F32) for ch, sl in enumerate(sls)]

    o = jnp.concatenate(outs, axis=0)
    ms = jnp.mean(o * o, axis=-1, keepdims=True)
    y = (o * lax.rsqrt(ms + EPS) * ng_ref[...]) * _silu(g_ref[0].astype(F32))
    o_ref[0] = y.astype(o_ref.dtype)


def _hgrn(proj, f32f, lb_raw, norm_g, layer):
    bsz, s, _ = proj.shape
    tr = min(T_REC, s)
    nl = lb_raw.shape[0]
    cq, ci, cg = 12, 20, 24
    kern = functools.partial(_hgrn_kernel, tr=tr, layer=layer)
    tw = min(REC_TRI, tr)
    r = jnp.arange(tw)
    tri = ((r[:, None] // REC_CHUNK == r[None, :] // REC_CHUNK) & (r[None, :] <= r[:, None])).astype(BF16)
    return pl.pallas_call(
        kern,
        grid=(bsz, REC_HEADS, s // tr),
        in_specs=[pl.BlockSpec((nl, HEAD_W), lambda b, h, i: (0, h)),
                  pl.BlockSpec((tw, tw), lambda b, h, i: (0, 0)),
                  pl.BlockSpec((1, tr, HEAD_W), lambda b, h, i: (b, i, cq + h)),
                  pl.BlockSpec((1, tr, HEAD_W), lambda b, h, i: (b, i, h)),
                  pl.BlockSpec((1, tr, HEAD_W), lambda b, h, i: (b, i, ci + h)),
                  pl.BlockSpec((1, tr, HEAD_W), lambda b, h, i: (b, i, cg + h)),
                  pl.BlockSpec((1, HEAD_W), lambda b, h, i: (0, h))],
        out_specs=pl.BlockSpec((1, tr, HEAD_W), lambda b, h, i: (b, i, h)),
        out_shape=jax.ShapeDtypeStruct((bsz, s, REC_HEADS * HEAD_W), BF16),
        scratch_shapes=[pltpu.VMEM((HEAD_W, HEAD_W), F32)],
        compiler_params=_cparams(("arbitrary", "arbitrary", "arbitrary")),
        name="hgrn",
    )(lb_raw, tri, proj, f32f, proj, proj, norm_g.reshape(1, -1))


def _pack_bf16_pairs(x):
    n = x.shape[1] // 2
    lo = pltpu.bitcast(x[:, :n].astype(BF16).astype(F32), jnp.uint32)
    hi = pltpu.bitcast(x[:, n:].astype(BF16).astype(F32), jnp.uint32)
    return (hi & jnp.uint32(0xFFFF0000)) | (lo >> 16)


def _join_bf16_pairs(lo, hi):
    return pltpu.bitcast(hi, jnp.uint32) | (pltpu.bitcast(lo, jnp.uint32) >> 16)


def _unpack_bf16_pairs(w):
    lo = pltpu.bitcast(w << 16, F32)
    hi = pltpu.bitcast(w & jnp.uint32(0xFFFF0000), F32)
    return lo, hi


def _outproj_kernel(ya_ref, yr_ref, x_ref, mod_ref, g_ref, wo_ref, wr_ref, br_ref,
                    x1_ref, h2_ref, meta_ref, metat_ref, cnt_ref, *, tm):
    half = ya_ref.shape[2]
    mix = (jnp.dot(ya_ref[0], wo_ref[0:half, :], preferred_element_type=F32)
           + jnp.dot(yr_ref[0], wo_ref[half:2 * half, :], preferred_element_type=F32))
    x1 = x_ref[0] + mod_ref[0, 2:3, :] * mix
    x1_ref[0] = x1
    h2 = _norm_mod(x1, g_ref[...], mod_ref[0, 4:5, :], mod_ref[0, 3:4, :])
    h2_ref[0] = _pack_bf16_pairs(h2)

    logits = jnp.dot(h2.astype(BF16), wr_ref[...], preferred_element_type=F32) + br_ref[...]
    lane = lax.broadcasted_iota(jnp.int32, logits.shape, 1)
    lane_f = lane.astype(F32)

    def first_lane(mask):
        return jnp.min(jnp.where(mask, lane_f, float(LANES)), axis=-1, keepdims=True).astype(jnp.int32)

    is_g = lane < N_GROUPS
    gl = jnp.where(is_g, logits, NEG)
    gmax = jnp.max(gl, axis=-1, keepdims=True)
    gsum = jnp.sum(jnp.exp(gl - gmax), axis=-1, keepdims=True)
    g_gate = 1.0 / gsum
    g_idx = first_lane(is_g & (gl == gmax))
    e_lo = N_GROUPS + g_idx * EXPERTS_PER_GROUP
    in_grp = (lane >= e_lo) & (lane < e_lo + EXPERTS_PER_GROUP)
    el = jnp.where(in_grp, logits, NEG)
    m1 = jnp.max(el, axis=-1, keepdims=True)
    i1 = first_lane(in_grp & (el == m1))
    el2 = jnp.where(lane == i1, NEG, el)
    m2 = jnp.max(el2, axis=-1, keepdims=True)
    i2 = first_lane(in_grp & (el2 == m2))
    r = jnp.exp(m2 - m1)
    w1 = g_gate / (1.0 + r)
    w2 = g_gate * r / (1.0 + r)
    e1 = i1 - N_GROUPS
    e2 = i2 - N_GROUPS

    oh1 = (lane == e1).astype(F32)
    oh2 = (lane == e2).astype(F32)
    oh = oh1 + oh2
    ri = lax.broadcasted_iota(jnp.int32, (tm, tm), 0)
    ci = lax.broadcasted_iota(jnp.int32, (tm, tm), 1)
    tri = (ci < ri).astype(BF16)
    rank = jnp.dot(tri, oh.astype(BF16), preferred_element_type=F32)
    cnt = jnp.sum(oh, axis=0, keepdims=True)
    cnt_up = jnp.floor((cnt + (GROUP - 1.0)) * (1.0 / GROUP)) * GROUP
    li = lax.broadcasted_iota(jnp.int32, (LANES, LANES), 0)
    lj = lax.broadcasted_iota(jnp.int32, (LANES, LANES), 1)
    before = (li < lj).astype(BF16)
    lstart = jnp.dot(jnp.broadcast_to(cnt_up, (8, LANES)).astype(BF16), before,
                     preferred_element_type=F32)[0:1, :]
    pos = rank + lstart
    p1 = jnp.sum(pos * oh1, axis=-1, keepdims=True)
    p2 = jnp.sum(pos * oh2, axis=-1, keepdims=True)
    vals = (p1, p2, w1, w2)
    meta = jnp.zeros(logits.shape, F32)
    for i, v in enumerate(vals):
        meta = jnp.where(lane == i, v, meta)
    meta_ref[0] = meta
    metat_ref[0] = meta.T[0:8, :]
    cnt_ref[0] = jnp.broadcast_to(cnt, (8, LANES))


def _outproj(ya, yr, x, mod6, g2n, wo_bf, wr_bf, br):
    bsz, s, d = x.shape
    tm = min(TM_PROJ, s)
    half = ya.shape[2]
    kern = functools.partial(_outproj_kernel, tm=tm)
    tok = lambda b, i: (b, i, 0)
    nt = s // tm
    tile = lambda b, i: (b * nt + i, 0, 0)
    return pl.pallas_call(
        kern,
        grid=(bsz, nt),
        in_specs=[pl.BlockSpec((1, tm, half), tok),
                  pl.BlockSpec((1, tm, half), tok),
                  pl.BlockSpec((1, tm, d), tok),
                  pl.BlockSpec((1, 6, d), lambda b, i: (b, 0, 0)),
                  pl.BlockSpec((1, d), lambda b, i: (0, 0)),
                  pl.BlockSpec((2 * half, d), lambda b, i: (0, 0)),
                  pl.BlockSpec((d, LANES), lambda b, i: (0, 0)),
                  pl.BlockSpec((1, LANES), lambda b, i: (0, 0))],
        out_specs=[pl.BlockSpec((1, tm, d), tok),
                   pl.BlockSpec((1, tm, d // 2), tok),
                   pl.BlockSpec((1, tm, LANES), tok),
                   pl.BlockSpec((1, 8, tm), tile),
                   pl.BlockSpec((1, 8, LANES), tile)],
        out_shape=[jax.ShapeDtypeStruct((bsz, s, d), F32),
                   jax.ShapeDtypeStruct((bsz, s, d // 2), jnp.uint32),
                   jax.ShapeDtypeStruct((bsz, s, LANES), F32),
                   jax.ShapeDtypeStruct((bsz * nt, 8, tm), F32),
                   jax.ShapeDtypeStruct((bsz * nt, 8, LANES), F32)],
        compiler_params=_cparams(("arbitrary", "arbitrary")),
        name="outproj",
    )(ya, yr, x, mod6, g2n.reshape(1, d), wo_bf, wr_bf, br)


GROUP = 8
DMA_UNROLL = 8


def _local_groups(tm):
    need = (tm * TOP_K + N_EXPERTS * (GROUP - 1)) // GROUP
    return -(-need // DMA_UNROLL) * DMA_UNROLL


def _dispatch_kernel(gdst_ref, metat_ref, h_ref, rows_ref, sort_ref, sem, *, tm, ng):
    tile = pl.program_id(0)
    last = pl.num_programs(0) - 1
    slot = tile % 2

    def drain(s):
        pltpu.make_async_copy(sort_ref.at[s], rows_ref.at[pl.ds(0, ng)], sem.at[s]).wait()

    @pl.when(tile >= 2)
    def _():
        drain(slot)

    p1 = metat_ref[0, 0:1, :].astype(jnp.int32)
    p2 = metat_ref[0, 1:2, :].astype(jnp.int32)
    r = lax.broadcasted_iota(jnp.int32, (ng * GROUP, tm), 0)
    perm = ((r == p1) | (r == p2)).astype(BF16)
    lo, hi = _unpack_bf16_pairs(h_ref[...])
    slo = jnp.dot(perm, lo.astype(BF16), preferred_element_type=F32)
    shi = jnp.dot(perm, hi.astype(BF16), preferred_element_type=F32)
    sort_ref[slot] = _join_bf16_pairs(slo, shi).reshape(sort_ref.shape[1:])

    def start(g, carry):
        dst = gdst_ref[tile * ng + g]
        pltpu.make_async_copy(sort_ref.at[slot, g], rows_ref.at[dst], sem.at[slot]).start()
        return carry

    lax.fori_loop(0, ng, start, 0, unroll=DMA_UNROLL)

    @pl.when(tile == last)
    def _():
        @pl.when(tile >= 1)
        def _():
            drain(1 - slot)

        drain(slot)


def _dispatch(gdst, meta_t, h2p):
    n, dw = h2p.shape
    tm = meta_t.shape[2]
    ng = _local_groups(tm)
    kern = functools.partial(_dispatch_kernel, tm=tm, ng=ng)
    return pl.pallas_call(
        kern,
        grid_spec=pltpu.PrefetchScalarGridSpec(
            num_scalar_prefetch=1,
            grid=(n // tm,),
            in_specs=[pl.BlockSpec((1, 8, tm), lambda i, a: (i, 0, 0)),
                      pl.BlockSpec((tm, dw), lambda i, a: (i, 0))],
            out_specs=pl.BlockSpec(memory_space=pl.ANY),
            scratch_shapes=[pltpu.VMEM((2, ng, GROUP, dw), jnp.uint32),
                            pltpu.SemaphoreType.DMA((2,))]),
        out_shape=jax.ShapeDtypeStruct((n // tm * ng, GROUP, dw), jnp.uint32),
        compiler_params=_cparams(("arbitrary",)),
        name="dispatch",
    )(gdst, meta_t, h2p)


def _group_tables(tcnt, tm):
    ng = _local_groups(tm)
    g8 = (tcnt + (GROUP - 1)) // GROUP
    gcount = jnp.sum(g8, axis=0)
    gstart = jnp.cumsum(gcount) - gcount
    gtile = gstart[None, :] + jnp.cumsum(g8, axis=0) - g8
    lend = jnp.cumsum(g8, axis=1)
    lstart = lend - g8
    g = jnp.arange(ng, dtype=jnp.int32)
    exp_of = jnp.sum(lend[:, None, :] <= g[None, :, None], axis=-1)
    onehot = exp_of[..., None] == jnp.arange(N_EXPERTS, dtype=jnp.int32)
    base = jnp.sum(jnp.where(onehot, (gtile - lstart)[:, None, :], 0), axis=-1)
    n_used = lend[:, -1:]
    used = g[None, :] < n_used
    n_spare = ng - n_used
    spare0 = jnp.sum(gcount) + jnp.cumsum(n_spare, axis=0) - n_spare
    gdst = jnp.where(used, base + g[None, :], spare0 + g[None, :] - n_used).astype(jnp.int32).reshape(-1)
    gsrc = jnp.where(used, base + g[None, :], 0).astype(jnp.int32).reshape(-1)
    return gdst, gsrc, gcount * GROUP


def _expert_kernel(wblk_ref, wexp_ref, wlo_ref, whi_ref, rows_ref, w1_ref, w3_ref, w2_ref, y_ref,
                   w1b_ref, w3b_ref, w2b_ref, acc_ref, *, blk, n_work):
    w = pl.program_id(0)
    prev = jnp.maximum(w - 1, 0)
    nxt = jnp.minimum(w + 1, n_work - 1)
    new_exp = (w == 0) | (wexp_ref[w] != wexp_ref[prev])
    new_blk = (w == 0) | (wblk_ref[w] != wblk_ref[prev])
    last_of_blk = (w == n_work - 1) | (wblk_ref[w] != wblk_ref[nxt])
    lo = wlo_ref[w]
    hi = whi_ref[w]

    @pl.when(new_exp)
    def _():
        w1b_ref[...] = w1_ref[0].astype(BF16)
        w3b_ref[...] = w3_ref[0].astype(BF16)
        w2b_ref[...] = w2_ref[0].astype(BF16)

    full = (lo == 0) & (hi == blk)

    @pl.when(new_blk & jnp.logical_not(full))
    def _():
        acc_ref[...] = jnp.zeros(acc_ref.shape, F32)

    def ffn():
        xlo, xhi = _unpack_bf16_pairs(rows_ref[...])
        half = xlo.shape[1]
        xlo = xlo.astype(BF16)
        xhi = xhi.astype(BF16)
        a = (jnp.dot(xlo, w1b_ref[0:half, :], preferred_element_type=F32)
             + jnp.dot(xhi, w1b_ref[half:2 * half, :], preferred_element_type=F32))
        b = (jnp.dot(xlo, w3b_ref[0:half, :], preferred_element_type=F32)
             + jnp.dot(xhi, w3b_ref[half:2 * half, :], preferred_element_type=F32))
        hmid = (_silu(a) * b).astype(BF16)
        return jnp.dot(hmid, w2b_ref[...], preferred_element_type=F32)

    @pl.when(full)
    def _():
        acc_ref[...] = ffn()

    @pl.when((hi > lo) & jnp.logical_not(full))
    def _():
        row = lax.broadcasted_iota(jnp.int32, (blk, 1), 0)
        keep = (row >= lo) & (row < hi)
        acc_ref[...] = acc_ref[...] + jnp.where(keep, ffn(), 0.0)

    @pl.when(last_of_blk)
    def _():
        y_ref[...] = _pack_bf16_pairs(acc_ref[...])


def _experts(wblk, wexp, wlo, whi, rows, a_rows, w1, w3, w2):
    dw = rows.shape[1]
    n_work = wblk.shape[0]
    blk = math.gcd(BLK_E, a_rows)
    _, d, ff = w1.shape
    kern = functools.partial(_expert_kernel, blk=blk, n_work=n_work)
    return pl.pallas_call(
        kern,
        grid_spec=pltpu.PrefetchScalarGridSpec(
            num_scalar_prefetch=4,
            grid=(n_work,),
            in_specs=[pl.BlockSpec((blk, dw), lambda w, wb, we, wl, wh: (wb[w], 0)),
                      pl.BlockSpec((1, d, ff), lambda w, wb, we, wl, wh: (we[w], 0, 0)),
                      pl.BlockSpec((1, d, ff), lambda w, wb, we, wl, wh: (we[w], 0, 0)),
                      pl.BlockSpec((1, ff, d), lambda w, wb, we, wl, wh: (we[w], 0, 0))],
            out_specs=pl.BlockSpec((blk, dw), lambda w, wb, we, wl, wh: (wb[w], 0)),
            scratch_shapes=[pltpu.VMEM((d, ff), BF16),
                            pltpu.VMEM((d, ff), BF16),
                            pltpu.VMEM((ff, d), BF16),
                            pltpu.VMEM((blk, d), F32)]),
        out_shape=jax.ShapeDtypeStruct((a_rows, dw), jnp.uint32),
        compiler_params=_cparams(("arbitrary",)),
        name="experts",
    )(wblk, wexp, wlo, whi, rows, w1, w3, w2)


def _work_items(counts, a_rows, blk):
    n_blk = a_rows // blk
    ends = jnp.cumsum(counts)
    starts = ends - counts
    total = ends[-1]
    pts = jnp.sort(jnp.concatenate([jnp.arange(n_blk, dtype=jnp.int32) * blk, starts[1:]]))
    nxt = jnp.concatenate([pts[1:], jnp.array([a_rows], jnp.int32)])
    blk_of = jnp.minimum(pts // blk, n_blk - 1)
    wexp = jnp.minimum(jnp.sum(ends[None, :] <= pts[:, None], axis=1), N_EXPERTS - 1).astype(jnp.int32)
    lo = pts - blk_of * blk
    hi = jnp.minimum(jnp.minimum(nxt, total), (blk_of + 1) * blk) - blk_of * blk
    hi = jnp.maximum(hi, lo)
    return blk_of.astype(jnp.int32), wexp, lo.astype(jnp.int32), hi.astype(jnp.int32)


def _combine_kernel(gsrc_ref, y_ref, x1_ref, meta_ref, mod_ref, g_ref, o_ref, sort_ref, sem, *, tm, ng):
    tile = pl.program_id(0) * pl.num_programs(1) + pl.program_id(1)
    n_tiles = pl.num_programs(0) * pl.num_programs(1)
    slot = tile % 2

    def fetch(t, s):
        def start(g, carry):
            src = gsrc_ref[t * ng + g]
            pltpu.make_async_copy(y_ref.at[src], sort_ref.at[s, g], sem.at[s]).start()
            return carry

        lax.fori_loop(0, ng, start, 0, unroll=DMA_UNROLL)

    @pl.when(tile == 0)
    def _():
        fetch(tile, slot)

    @pl.when(tile + 1 < n_tiles)
    def _():
        fetch(tile + 1, 1 - slot)

    pltpu.make_async_copy(y_ref.at[pl.ds(0, ng)], sort_ref.at[slot], sem.at[slot]).wait()

    meta = meta_ref[0]
    nrow = ng * GROUP
    lo, hi = _unpack_bf16_pairs(sort_ref[slot].reshape(nrow, sort_ref.shape[3]))
    lo = lo.astype(BF16)
    hi = hi.astype(BF16)
    col = lax.broadcasted_iota(jnp.int32, (tm, nrow), 1)
    mix = jnp.zeros((tm, nrow), F32)
    for kk in range(TOP_K):
        mix = jnp.where(col == meta[:, kk:kk + 1].astype(jnp.int32), meta[:, TOP_K + kk:TOP_K + kk + 1], mix)
    mix = mix.astype(BF16)
    moe = jnp.concatenate([jnp.dot(mix, lo, preferred_element_type=F32),
                           jnp.dot(mix, hi, preferred_element_type=F32)], axis=1)
    x2 = x1_ref[0] + mod_ref[0, 5:6, :] * moe
    ms = jnp.mean(x2 * x2, axis=-1, keepdims=True)
    o_ref[0] = x2 * lax.rsqrt(ms + EPS) * g_ref[...]


def _combine(gsrc, ybuf3, x1, meta, mod6, gf):
    bsz, s, d = x1.shape
    tm = min(TM_PROJ, s)
    ng = _local_groups(tm)
    kern = functools.partial(_combine_kernel, tm=tm, ng=ng)
    tok = lambda b, i, a: (b, i, 0)
    return pl.pallas_call(
        kern,
        grid_spec=pltpu.PrefetchScalarGridSpec(
            num_scalar_prefetch=1,
            grid=(bsz, s // tm),
            in_specs=[pl.BlockSpec(memory_space=pl.ANY),
                      pl.BlockSpec((1, tm, d), tok),
                      pl.BlockSpec((1, tm, LANES), tok),
                      pl.BlockSpec((1, 6, d), lambda b, i, a: (b, 0, 0)),
                      pl.BlockSpec((1, d), lambda b, i, a: (0, 0))],
            out_specs=pl.BlockSpec((1, tm, d), tok),
            scratch_shapes=[pltpu.VMEM((2, ng, GROUP, d // 2), jnp.uint32),
                            pltpu.SemaphoreType.DMA((2,))]),
        out_shape=jax.ShapeDtypeStruct((bsz, s, d), F32),
        compiler_params=_cparams(("arbitrary", "arbitrary")),
        name="combine",
    )(gsrc, ybuf3, x1, meta, mod6, gf.reshape(1, d))


def kernel(x, c, w_ada, b_ada, norm1_g, norm2_g, w_in, attn_lambda_q1, attn_lambda_k1, attn_lambda_q2,
           attn_lambda_k2, attn_subln_g, rel_bias_table, rec_lower_bound, rec_norm_g, w_out, w_group,
           b_group, w_expert, b_expert, w1, w3, w2, final_norm_g):
    bsz, s, d = x.shape
    depth = w_ada.shape[0]
    n_tok = bsz * s
    for l in range(depth):
        mod6 = _mod(c, w_ada[l], b_ada[l]).reshape(bsz, 6, d)
        proj, f32f = _inproj(x, mod6, norm1_g[l], w_in[l].astype(BF16))
        lq4 = jnp.stack([attn_lambda_q1[l], attn_lambda_k1[l], attn_lambda_q2[l], attn_lambda_k2[l]])
        ya = _attention(proj, lq4, attn_subln_g[l], rel_bias_table, l)
        yr = _hgrn(proj, f32f, rec_lower_bound, rec_norm_g[l], l)
        w_r = jnp.zeros((d, LANES), F32).at[:, :N_GROUPS].set(w_group[l])
        w_r = w_r.at[:, N_GROUPS:N_GROUPS + N_EXPERTS].set(w_expert[l]).astype(BF16)
        b_r = jnp.zeros((1, LANES), F32).at[0, :N_GROUPS].set(b_group[l])
        b_r = b_r.at[0, N_GROUPS:N_GROUPS + N_EXPERTS].set(b_expert[l])
        x1, h2p, meta, meta_t, cnt = _outproj(ya, yr, x, mod6, norm2_g[l], w_out[l].astype(BF16), w_r, b_r)
        tm = meta_t.shape[2]
        tcnt = cnt[:, 0, :N_EXPERTS].astype(jnp.int32)
        gdst, gsrc, counts_up = _group_tables(tcnt, tm)
        rows3 = _dispatch(gdst, meta_t, h2p.reshape(n_tok, d // 2))
        buf_rows = rows3.shape[0] * GROUP
        blk = math.gcd(BLK_E, buf_rows)
        wblk, wexp, wlo, whi = _work_items(counts_up, buf_rows, blk)
        ybuf = _experts(wblk, wexp, wlo, whi, rows3.reshape(buf_rows, d // 2), buf_rows, w1[l], w3[l], w2[l])
        gf = final_norm_g if l == depth - 1 else jnp.ones((d,), F32)
        assert depth == 1
        x = _combine(gsrc, ybuf.reshape(buf_rows // GROUP, GROUP, d // 2), x1, meta, mod6, gf)
    return x
```

```python
import functools
import math

import jax
import jax.numpy as jnp
from jax import lax
from jax.experimental import pallas as pl
from jax.experimental.pallas import tpu as pltpu

F32 = jnp.float32
BF16 = jnp.bfloat16
EPS = 1e-6

ATTN_HEADS = 4
ATTN_HEAD_DIM = 64
HEAD_W = 128
REC_HEADS = 4
REC_CHUNK = 64
NUM_BUCKETS = 32
MAX_DISTANCE = 128
N_GROUPS = 4
EXPERTS_PER_GROUP = 8
N_EXPERTS = 32
TOP_K = 2
NEG = -1e30

LANES = 128
VMEM_LIMIT = 56 * 1024 * 1024

TM_PROJ = 512
T_ATTN = 512
ATTN_COLS = 256
T_REC = 512
REC_TRI = 256
BLK_E = 512


def _cparams(sem):
    return pltpu.CompilerParams(dimension_semantics=sem, vmem_limit_bytes=VMEM_LIMIT)


def _sigmoid(x):
    return 0.5 * jnp.tanh(0.5 * x) + 0.5


def _silu(x):
    return x * _sigmoid(x)


def _mod_kernel(c_ref, w_ref, b_ref, o_ref):
    ca = _silu(c_ref[...])
    o_ref[...] = jnp.dot(ca, w_ref[...], preferred_element_type=F32,
                         precision=lax.Precision.HIGHEST) + b_ref[...]


def _mod(c, w, b):
    bsz, d = c.shape
    n = w.shape[1]
    tn = 1024
    return pl.pallas_call(
        _mod_kernel,
        grid=(n // tn,),
        in_specs=[pl.BlockSpec((bsz, d), lambda j: (0, 0)),
                  pl.BlockSpec((d, tn), lambda j: (0, j)),
                  pl.BlockSpec((1, tn), lambda j: (0, j))],
        out_specs=pl.BlockSpec((bsz, tn), lambda j: (0, j)),
        out_shape=jax.ShapeDtypeStruct((bsz, n), F32),
        compiler_params=_cparams(("arbitrary",)),
        name="mod",
    )(c, w, b.reshape(1, n))


def _norm_mod(x, g, sc, sh):
    ms = jnp.mean(x * x, axis=-1, keepdims=True)
    return (x * lax.rsqrt(ms + EPS) * g) * (1.0 + sc) + sh


def _inproj_kernel(x_ref, mod_ref, g_ref, w_ref, proj_ref, f_ref, *, f_chunk, n_chunks, cw):
    h = _norm_mod(x_ref[0], g_ref[...], mod_ref[0, 1:2, :], mod_ref[0, 0:1, :]).astype(BF16)
    for c in range(n_chunks):
        r = jnp.dot(h, w_ref[:, c * cw:(c + 1) * cw], preferred_element_type=F32)
        proj_ref[0, :, c * cw:(c + 1) * cw] = r.astype(BF16)
        if c == f_chunk:
            f_ref[0] = r


def _inproj(x, mod6, g, w_bf):
    bsz, s, d = x.shape
    n = w_bf.shape[1]
    cw = 512
    tm = min(TM_PROJ, s)
    kern = functools.partial(_inproj_kernel, f_chunk=4, n_chunks=n // cw, cw=cw)
    return pl.pallas_call(
        kern,
        grid=(bsz, s // tm),
        in_specs=[pl.BlockSpec((1, tm, d), lambda b, i: (b, i, 0)),
                  pl.BlockSpec((1, 6, d), lambda b, i: (b, 0, 0)),
                  pl.BlockSpec((1, d), lambda b, i: (0, 0)),
                  pl.BlockSpec((d, n), lambda b, i: (0, 0))],
        out_specs=[pl.BlockSpec((1, tm, n), lambda b, i: (b, i, 0)),
                   pl.BlockSpec((1, tm, cw), lambda b, i: (b, i, 0))],
        out_shape=[jax.ShapeDtypeStruct((bsz, s, n), BF16),
                   jax.ShapeDtypeStruct((bsz, s, cw), F32)],
        compiler_params=_cparams(("arbitrary", "arbitrary")),
        name="inproj",
    )(x, mod6, g.reshape(1, d), w_bf)


LOG2E = math.log2(math.e)


def _attn_kernel(lq_ref, q_ref, k_ref, v_ref, d_ref, g_ref, o_ref,
                 qs_ref, vt_ref, s_ref, m_ref, l_ref, acc_ref, *, t, lam_init):
    qi = pl.program_id(2)
    nkt = vt_ref.shape[0]

    @pl.when(qi == 0)
    def _():
        for c in range(nkt):
            vt_ref[c] = v_ref[0, c * t:(c + 1) * t, :].T

    q = q_ref[0]
    lane = lax.broadcasted_iota(jnp.int32, q.shape, 1)
    qf = q.astype(F32) * (ATTN_HEAD_DIM ** -0.5 * LOG2E)
    zero = jnp.zeros_like(qf)
    qs_ref[0:t, :] = jnp.where(lane < ATTN_HEAD_DIM, qf, zero).astype(BF16)
    qs_ref[t:2 * t, :] = jnp.where(lane >= ATTN_HEAD_DIM, qf, zero).astype(BF16)
    m_ref[...] = jnp.full(m_ref.shape, NEG, F32)
    l_ref[...] = jnp.zeros(l_ref.shape, F32)
    acc_ref[...] = jnp.zeros(acc_ref.shape, F32)

    blocks = [slice(c * ATTN_COLS, (c + 1) * ATTN_COLS) for c in range(2 * t // ATTN_COLS)]

    def scores(j, cs):
        k = k_ref[0, pl.ds(pl.multiple_of(j * t, t), t), :]
        return lax.dot_general(k, qs_ref[cs, :], (((1,), (1,)), ((), ())), preferred_element_type=F32)

    def tile(j, bias_ref, j_next):
        vt = vt_ref[j]
        m_all = m_ref[...]
        l_all = l_ref[...]
        ps, alphas, m_news, l_news = [], [], [], []
        for cs in blocks:
            s = s_ref[:, cs]
            if bias_ref is not None:
                s = s + bias_ref[:, cs]
            m_old = m_all[:, cs]
            m_new = jnp.maximum(m_old, jnp.max(s, axis=0, keepdims=True))
            alpha = jnp.exp2(m_old - m_new)
            p = jnp.exp2(s - m_new)
            l_news.append(alpha * l_all[:, cs] + jnp.sum(p, axis=0, keepdims=True))
            ps.append(p.astype(BF16))
            alphas.append(alpha)
            m_news.append(m_new)
            if j_next is not None:
                s_ref[:, cs] = scores(j_next, cs)
        pvs = [jnp.dot(vt, p, preferred_element_type=F32) for p in ps]
        for cs, alpha, pv in zip(blocks, alphas, pvs):
            acc_ref[:, cs] = alpha * acc_ref[:, cs] + pv
        m_ref[...] = jnp.concatenate(m_news, axis=1)
        l_ref[...] = jnp.concatenate(l_news, axis=1)

    for cs in blocks:
        s_ref[:, cs] = scores(0, cs)

    n_far = jnp.maximum(qi - 1, 0)

    def far_pair(jj, carry):
        tile(2 * jj, None, 2 * jj + 1)
        tile(2 * jj + 1, None, 2 * jj + 2)
        return carry

    lax.fori_loop(0, n_far // 2, far_pair, 0)

    @pl.when(n_far % 2 == 1)
    def _():
        tile(n_far - 1, None, n_far)

    @pl.when(qi >= 1)
    def _():
        tile(qi - 1, d_ref.at[0, 1], qi)

    tile(qi, d_ref.at[0, 0], None)

    lq = lq_ref[...]
    lam = (jnp.exp(jnp.sum(lq[0:1] * lq[1:2], axis=-1, keepdims=True))
           - jnp.exp(jnp.sum(lq[2:3] * lq[3:4], axis=-1, keepdims=True)) + lam_init)
    o = acc_ref[...] / l_ref[...]
    a = o[:, 0:t] - lam * o[:, t:2 * t]
    ms = jnp.mean(a * a, axis=0, keepdims=True)
    y = (a * lax.rsqrt(ms + EPS)).T * (g_ref[...] * (1.0 - lam_init))
    o_ref[0] = y.astype(o_ref.dtype)


def _t5_bucket(rel):
    n = jnp.maximum(rel, 0)
    max_exact = NUM_BUCKETS // 2
    nf = jnp.maximum(n, 1).astype(F32)
    large = max_exact + (jnp.log(nf / max_exact) / math.log(MAX_DISTANCE / max_exact)
                         * (NUM_BUCKETS - max_exact)).astype(jnp.int32)
    large = jnp.minimum(large, NUM_BUCKETS - 1)
    return jnp.where(n < max_exact, n, large)


def _bias_tiles(rel_table, t):
    assert t >= MAX_DISTANCE
    nh = rel_table.shape[1]
    rel = jnp.arange(2 * t)
    vec = (rel_table[_t5_bucket(rel)].astype(F32) - rel_table[NUM_BUCKETS - 1].astype(F32)[None, :]).T * LOG2E
    neg = jnp.full((nh, t), NEG, F32)

    def toeplitz(w):
        flat = jnp.tile(w, (1, t))[:, :t * (2 * t - 1)]
        return flat.reshape(nh, t, 2 * t - 1)[:, :, :t]

    diag = toeplitz(jnp.concatenate([vec[:, :t], neg], axis=1))
    prev = toeplitz(jnp.concatenate([vec[:, t:], vec[:, :t]], axis=1))
    both = jnp.stack([diag, prev], axis=1)
    return jnp.concatenate([both, both], axis=-1)


def _attention(proj, lq4, subln_g, rel_table, layer):
    bsz, s, _ = proj.shape
    t = min(T_ATTN, s)
    nq = s // t
    lam_init = 0.8 - 0.6 * math.exp(-0.3 * layer)
    dt = _bias_tiles(rel_table, t)
    hq, hk, hv = 0, ATTN_HEADS, 2 * ATTN_HEADS
    kern = functools.partial(_attn_kernel, t=t, lam_init=lam_init)
    return pl.pallas_call(
        kern,
        grid=(bsz, ATTN_HEADS, nq),
        in_specs=[pl.BlockSpec((4, ATTN_HEAD_DIM), lambda b, h, i: (0, 0)),
                  pl.BlockSpec((1, t, HEAD_W), lambda b, h, i: (b, i, hq + h)),
                  pl.BlockSpec((1, s, HEAD_W), lambda b, h, i: (b, 0, hk + h)),
                  pl.BlockSpec((1, s, HEAD_W), lambda b, h, i: (b, 0, hv + h)),
                  pl.BlockSpec((1, 2, t, 2 * t), lambda b, h, i: (h, 0, 0, 0)),
                  pl.BlockSpec((1, HEAD_W), lambda b, h, i: (0, 0))],
        out_specs=pl.BlockSpec((1, t, HEAD_W), lambda b, h, i: (b, i, h)),
        out_shape=jax.ShapeDtypeStruct((bsz, s, ATTN_HEADS * HEAD_W), BF16),
        scratch_shapes=[pltpu.VMEM((2 * t, HEAD_W), BF16),
                        pltpu.VMEM((nq, HEAD_W, t), BF16),
                        pltpu.VMEM((t, 2 * t), F32),
                        pltpu.VMEM((1, 2 * t), F32),
                        pltpu.VMEM((1, 2 * t), F32),
                        pltpu.VMEM((HEAD_W, 2 * t), F32)],
        compiler_params=_cparams(("arbitrary", "arbitrary", "arbitrary")),
        name="attn",
    )(lq4, proj, proj, proj, dt, subln_g.reshape(1, HEAD_W))


def _hgrn_kernel(lb_ref, tri_ref, q_ref, f_ref, i_ref, g_ref, ng_ref, o_ref, st_ref, *, tr, layer):
    c = REC_CHUNK
    nch = tr // c

    @pl.when(pl.program_id(2) == 0)
    def _():
        st_ref[...] = jnp.zeros(st_ref.shape, F32)

    lbr = lb_ref[...]
    e = jnp.exp(lbr - jnp.max(lbr, axis=0, keepdims=True))
    lb = jnp.sum(e[0:layer + 1], axis=0, keepdims=True) / jnp.sum(e, axis=0, keepdims=True)

    fr = f_ref[0]
    sg = _sigmoid(fr)
    lf = jnp.log2(lb + (1.0 - lb) * sg)
    kf = (1.0 - lb) * (1.0 - sg)
    qf = _silu(q_ref[0].astype(F32))
    vb = i_ref[0]

    row = lax.broadcasted_iota(jnp.int32, (tr, LANES), 0)

    lf_hi = lf.astype(BF16)
    lf_lo = (lf - lf_hi.astype(F32)).astype(BF16)
    lf2 = jnp.concatenate([lf_hi, lf_lo], axis=1)
    tw = tri_ref.shape[0]
    parts = []
    for r0 in range(0, tr, tw):
        cs2 = jnp.dot(tri_ref[...], lf2[r0:r0 + tw], preferred_element_type=F32)
        parts.append(cs2[:, :LANES] + cs2[:, LANES:])
    b = jnp.concatenate(parts, axis=0)

    kf3 = kf.reshape(tr // 8, 8, LANES)
    b3 = b.reshape(tr // 8, 8, LANES)
    diags = [jnp.sum(qf * kf, axis=-1, keepdims=True)]
    for dlt in range(1, 8):
        kd = pltpu.roll(kf3, dlt, axis=1).reshape(tr, LANES)
        bd = pltpu.roll(b3, dlt, axis=1).reshape(tr, LANES)
        term = qf * kd * jnp.exp2(b - bd)
        diags.append(jnp.sum(term, axis=-1, keepdims=True))
    lv_q, lv_k = [], []
    for w in (8, 16, 32):
        bend = jnp.broadcast_to(b.reshape(tr // w, w, LANES)[:, w - 1:w, :],
                                (tr // w, w, LANES)).reshape(tr, LANES)
        right = (row // w) % 2 == 1
        pend = pltpu.roll(bend, w, axis=0)
        lv_q.append(jnp.where(right, qf * jnp.exp2(jnp.minimum(b - pend, 0.0)), 0.0).astype(BF16))
        lv_k.append(jnp.where(right, 0.0, kf * jnp.exp2(jnp.minimum(bend - b, 0.0))).astype(BF16))

    blast = jnp.broadcast_to(b.reshape(nch, c, LANES)[:, c - 1:c, :], (nch, c, LANES)).reshape(tr, LANES)
    qdec = (qf * jnp.exp2(b)).astype(BF16)
    kdec = (kf * jnp.exp2(blast - b)).astype(BF16)
    dec = jnp.exp2(blast)

    ri = lax.broadcasted_iota(jnp.int32, (c, c), 0)
    ci = lax.broadcasted_iota(jnp.int32, (c, c), 1)
    nt = (((1,), (1,)), ((), ()))
    sls = [slice(ch * c, (ch + 1) * c) for ch in range(nch)]
    sws = [[lax.dot_general(lv_q[li][sl], lv_k[li][sl], nt, preferred_element_type=F32) for li in range(3)]
           for sl in sls]
    upds = [lax.dot_general(vb[sl], kdec[sl], (((0,), (0,)), ((), ())), preferred_element_type=F32)
            for sl in sls]
    sames = [ri // (2 * w) == ci // (2 * w) for w in (8, 16, 32)]
    dsel = [(ci == ri - dlt) & (ri % 8 >= dlt) for dlt in range(8)]
    a_bf = []
    for ch, sl in enumerate(sls):
        a = jnp.zeros((c, c), F32)
        for li in range(3):
            a = a + jnp.where(sames[li], sws[ch][li], 0.0)
        for dlt in range(8):
            a = a + jnp.where(dsel[dlt], diags[dlt][sl], 0.0)
        a_bf.append(a.astype(BF16))
    st = st_ref[...]
    sts = []
    for ch in range(nch):
        sts.append(st.astype(BF16))
        st = dec[ch * c:ch * c + 1, :] * st + upds[ch]
    st_ref[...] = st
    outs = [lax.dot_general(qdec[sl], sts[ch], nt, preferred_element_type=F32)
            + jnp.dot(a_bf[ch], vb[sl], preferred_element_type=F32) for ch, sl in enumerate(sls)]

    o = jnp.concatenate(outs, axis=0)
    ms = jnp.mean(o * o, axis=-1, keepdims=True)
    y = (o * lax.rsqrt(ms + EPS) * ng_ref[...]) * _silu(g_ref[0].astype(F32))
    o_ref[0] = y.astype(o_ref.dtype)


def _hgrn(proj, f32f, lb_raw, norm_g, layer):
    bsz, s, _ = proj.shape
    tr = min(T_REC, s)
    nl = lb_raw.shape[0]
    cq, ci, cg = 12, 20, 24
    kern = functools.partial(_hgrn_kernel, tr=tr, layer=layer)
    tw = min(REC_TRI, tr)
    r = jnp.arange(tw)
    tri = ((r[:, None] // REC_CHUNK == r[None, :] // REC_CHUNK) & (r[None, :] <= r[:, None])).astype(BF16)
    return pl.pallas_call(
        kern,
        grid=(bsz, REC_HEADS, s // tr),
        in_specs=[pl.BlockSpec((nl, HEAD_W), lambda b, h, i: (0, h)),
                  pl.BlockSpec((tw, tw), lambda b, h, i: (0, 0)),
                  pl.BlockSpec((1, tr, HEAD_W), lambda b, h, i: (b, i, cq + h)),
                  pl.BlockSpec((1, tr, HEAD_W), lambda b, h, i: (b, i, h)),
                  pl.BlockSpec((1, tr, HEAD_W), lambda b, h, i: (b, i, ci + h)),
                  pl.BlockSpec((1, tr, HEAD_W), lambda b, h, i: (b, i, cg + h)),
                  pl.BlockSpec((1, HEAD_W), lambda b, h, i: (0, h))],
        out_specs=pl.BlockSpec((1, tr, HEAD_W), lambda b, h, i: (b, i, h)),
        out_shape=jax.ShapeDtypeStruct((bsz, s, REC_HEADS * HEAD_W), BF16),
        scratch_shapes=[pltpu.VMEM((HEAD_W, HEAD_W), F32)],
        compiler_params=_cparams(("arbitrary", "arbitrary", "arbitrary")),
        name="hgrn",
    )(lb_raw, tri, proj, f32f, proj, proj, norm_g.reshape(1, -1))


def _pack_bf16_pairs(x):
    n = x.shape[1] // 2
    lo = pltpu.bitcast(x[:, :n].astype(BF16).astype(F32), jnp.uint32)
    hi = pltpu.bitcast(x[:, n:].astype(BF16).astype(F32), jnp.uint32)
    return (hi & jnp.uint32(0xFFFF0000)) | (lo >> 16)


def _join_bf16_pairs(lo, hi):
    return pltpu.bitcast(hi, jnp.uint32) | (pltpu.bitcast(lo, jnp.uint32) >> 16)


def _unpack_bf16_pairs(w):
    lo = pltpu.bitcast(w << 16, F32)
    hi = pltpu.bitcast(w & jnp.uint32(0xFFFF0000), F32)
    return lo, hi


def _outproj_kernel(ya_ref, yr_ref, x_ref, mod_ref, g_ref, wo_ref, wr_ref, br_ref,
                    x1_ref, h2_ref, meta_ref, metat_ref, cnt_ref, *, tm):
    half = ya_ref.shape[2]
    mix = (jnp.dot(ya_ref[0], wo_ref[0:half, :], preferred_element_type=F32)
           + jnp.dot(yr_ref[0], wo_ref[half:2 * half, :], preferred_element_type=F32))
    x1 = x_ref[0] + mod_ref[0, 2:3, :] * mix
    x1_ref[0] = x1
    h2 = _norm_mod(x1, g_ref[...], mod_ref[0, 4:5, :], mod_ref[0, 3:4, :])
    h2_ref[0] = _pack_bf16_pairs(h2)

    logits = jnp.dot(h2.astype(BF16), wr_ref[...], preferred_element_type=F32) + br_ref[...]
    lane = lax.broadcasted_iota(jnp.int32, logits.shape, 1)
    lane_f = lane.astype(F32)

    def first_lane(mask):
        return jnp.min(jnp.where(mask, lane_f, float(LANES)), axis=-1, keepdims=True).astype(jnp.int32)

    is_g = lane < N_GROUPS
    gl = jnp.where(is_g, logits, NEG)
    gmax = jnp.max(gl, axis=-1, keepdims=True)
    gsum = jnp.sum(jnp.exp(gl - gmax), axis=-1, keepdims=True)
    g_gate = 1.0 / gsum
    g_idx = first_lane(is_g & (gl == gmax))
    e_lo = N_GROUPS + g_idx * EXPERTS_PER_GROUP
    in_grp = (lane >= e_lo) & (lane < e_lo + EXPERTS_PER_GROUP)
    el = jnp.where(in_grp, logits, NEG)
    m1 = jnp.max(el, axis=-1, keepdims=True)
    i1 = first_lane(in_grp & (el == m1))
    el2 = jnp.where(lane == i1, NEG, el)
    m2 = jnp.max(el2, axis=-1, keepdims=True)
    i2 = first_lane(in_grp & (el2 == m2))
    r = jnp.exp(m2 - m1)
    w1 = g_gate / (1.0 + r)
    w2 = g_gate * r / (1.0 + r)
    e1 = i1 - N_GROUPS
    e2 = i2 - N_GROUPS

    oh1 = (lane == e1).astype(F32)
    oh2 = (lane == e2).astype(F32)
    oh = oh1 + oh2
    ri = lax.broadcasted_iota(jnp.int32, (tm, tm), 0)
    ci = lax.broadcasted_iota(jnp.int32, (tm, tm), 1)
    tri = (ci < ri).astype(BF16)
    rank = jnp.dot(tri, oh.astype(BF16), preferred_element_type=F32)
    cnt = jnp.sum(oh, axis=0, keepdims=True)
    cnt_up = jnp.floor((cnt + (GROUP - 1.0)) * (1.0 / GROUP)) * GROUP
    li = lax.broadcasted_iota(jnp.int32, (LANES, LANES), 0)
    lj = lax.broadcasted_iota(jnp.int32, (LANES, LANES), 1)
    before = (li < lj).astype(BF16)
    lstart = jnp.dot(jnp.broadcast_to(cnt_up, (8, LANES)).astype(BF16), before,
                     preferred_element_type=F32)[0:1, :]
    pos = rank + lstart
    p1 = jnp.sum(pos * oh1, axis=-1, keepdims=True)
    p2 = jnp.sum(pos * oh2, axis=-1, keepdims=True)
    vals = (p1, p2, w1, w2)
    meta = jnp.zeros(logits.shape, F32)
    for i, v in enumerate(vals):
        meta = jnp.where(lane == i, v, meta)
    meta_ref[0] = meta
    metat_ref[0] = meta.T[0:8, :]
    cnt_ref[0] = jnp.broadcast_to(cnt, (8, LANES))


def _outproj(ya, yr, x, mod6, g2n, wo_bf, wr_bf, br):
    bsz, s, d = x.shape
    tm = min(TM_PROJ, s)
    half = ya.shape[2]
    kern = functools.partial(_outproj_kernel, tm=tm)
    tok = lambda b, i: (b, i, 0)
    nt = s // tm
    tile = lambda b, i: (b * nt + i, 0, 0)
    return pl.pallas_call(
        kern,
        grid=(bsz, nt),
        in_specs=[pl.BlockSpec((1, tm, half), tok),
                  pl.BlockSpec((1, tm, half), tok),
                  pl.BlockSpec((1, tm, d), tok),
                  pl.BlockSpec((1, 6, d), lambda b, i: (b, 0, 0)),
                  pl.BlockSpec((1, d), lambda b, i: (0, 0)),
                  pl.BlockSpec((2 * half, d), lambda b, i: (0, 0)),
                  pl.BlockSpec((d, LANES), lambda b, i: (0, 0)),
                  pl.BlockSpec((1, LANES), lambda b, i: (0, 0))],
        out_specs=[pl.BlockSpec((1, tm, d), tok),
                   pl.BlockSpec((1, tm, d // 2), tok),
                   pl.BlockSpec((1, tm, LANES), tok),
                   pl.BlockSpec((1, 8, tm), tile),
                   pl.BlockSpec((1, 8, LANES), tile)],
        out_shape=[jax.ShapeDtypeStruct((bsz, s, d), F32),
                   jax.ShapeDtypeStruct((bsz, s, d // 2), jnp.uint32),
                   jax.ShapeDtypeStruct((bsz, s, LANES), F32),
                   jax.ShapeDtypeStruct((bsz * nt, 8, tm), F32),
                   jax.ShapeDtypeStruct((bsz * nt, 8, LANES), F32)],
        compiler_params=_cparams(("arbitrary", "arbitrary")),
        name="outproj",
    )(ya, yr, x, mod6, g2n.reshape(1, d), wo_bf, wr_bf, br)


GROUP = 8
DMA_UNROLL = 8


def _local_groups(tm):
    need = (tm * TOP_K + N_EXPERTS * (GROUP - 1)) // GROUP
    return -(-need // DMA_UNROLL) * DMA_UNROLL


def _dispatch_kernel(gdst_ref, metat_ref, h_ref, rows_ref, sort_ref, sem, *, tm, ng):
    tile = pl.program_id(0)
    last = pl.num_programs(0) - 1
    slot = tile % 2

    def drain(s):
        pltpu.make_async_copy(sort_ref.at[s], rows_ref.at[pl.ds(0, ng)], sem.at[s]).wait()

    @pl.when(tile >= 2)
    def _():
        drain(slot)

    p1 = metat_ref[0, 0:1, :].astype(jnp.int32)
    p2 = metat_ref[0, 1:2, :].astype(jnp.int32)
    r = lax.broadcasted_iota(jnp.int32, (ng * GROUP, tm), 0)
    perm = ((r == p1) | (r == p2)).astype(BF16)
    lo, hi = _unpack_bf16_pairs(h_ref[...])
    slo = jnp.dot(perm, lo.astype(BF16), preferred_element_type=F32)
    shi = jnp.dot(perm, hi.astype(BF16), preferred_element_type=F32)
    sort_ref[slot] = _join_bf16_pairs(slo, shi).reshape(sort_ref.shape[1:])

    def start(g, carry):
        dst = gdst_ref[tile * ng + g]
        pltpu.make_async_copy(sort_ref.at[slot, g], rows_ref.at[dst], sem.at[slot]).start()
        return carry

    lax.fori_loop(0, ng, start, 0, unroll=DMA_UNROLL)

    @pl.when(tile == last)
    def _():
        @pl.when(tile >= 1)
        def _():
            drain(1 - slot)

        drain(slot)


def _dispatch(gdst, meta_t, h2p):
    n, dw = h2p.shape
    tm = meta_t.shape[2]
    ng = _local_groups(tm)
    kern = functools.partial(_dispatch_kernel, tm=tm, ng=ng)
    return pl.pallas_call(
        kern,
        grid_spec=pltpu.PrefetchScalarGridSpec(
            num_scalar_prefetch=1,
            grid=(n // tm,),
            in_specs=[pl.BlockSpec((1, 8, tm), lambda i, a: (i, 0, 0)),
                      pl.BlockSpec((tm, dw), lambda i, a: (i, 0))],
            out_specs=pl.BlockSpec(memory_space=pl.ANY),
            scratch_shapes=[pltpu.VMEM((2, ng, GROUP, dw), jnp.uint32),
                            pltpu.SemaphoreType.DMA((2,))]),
        out_shape=jax.ShapeDtypeStruct((n // tm * ng, GROUP, dw), jnp.uint32),
        compiler_params=_cparams(("arbitrary",)),
        name="dispatch",
    )(gdst, meta_t, h2p)


def _group_tables(tcnt, tm):
    ng = _local_groups(tm)
    g8 = (tcnt + (GROUP - 1)) // GROUP
    gcount = jnp.sum(g8, axis=0)
    gstart = jnp.cumsum(gcount) - gcount
    gtile = gstart[None, :] + jnp.cumsum(g8, axis=0) - g8
    lend = jnp.cumsum(g8, axis=1)
    lstart = lend - g8
    g = jnp.arange(ng, dtype=jnp.int32)
    exp_of = jnp.sum(lend[:, None, :] <= g[None, :, None], axis=-1)
    onehot = exp_of[..., None] == jnp.arange(N_EXPERTS, dtype=jnp.int32)
    base = jnp.sum(jnp.where(onehot, (gtile - lstart)[:, None, :], 0), axis=-1)
    n_used = lend[:, -1:]
    used = g[None, :] < n_used
    n_spare = ng - n_used
    spare0 = jnp.sum(gcount) + jnp.cumsum(n_spare, axis=0) - n_spare
    gdst = jnp.where(used, base + g[None, :], spare0 + g[None, :] - n_used).astype(jnp.int32).reshape(-1)
    gsrc = jnp.where(used, base + g[None, :], 0).astype(jnp.int32).reshape(-1)
    return gdst, gsrc, gcount * GROUP


def _expert_kernel(wblk_ref, wexp_ref, wlo_ref, whi_ref, rows_ref, w1_ref, w3_ref, w2_ref, y_ref,
                   w1b_ref, w3b_ref, w2b_ref, acc_ref, *, blk, n_work):
    w = pl.program_id(0)
    prev = jnp.maximum(w - 1, 0)
    nxt = jnp.minimum(w + 1, n_work - 1)
    new_exp = (w == 0) | (wexp_ref[w] != wexp_ref[prev])
    new_blk = (w == 0) | (wblk_ref[w] != wblk_ref[prev])
    last_of_blk = (w == n_work - 1) | (wblk_ref[w] != wblk_ref[nxt])
    lo = wlo_ref[w]
    hi = whi_ref[w]

    @pl.when(new_exp)
    def _():
        w1b_ref[...] = w1_ref[0].astype(BF16)
        w3b_ref[...] = w3_ref[0].astype(BF16)
        w2b_ref[...] = w2_ref[0].astype(BF16)

    full = (lo == 0) & (hi == blk)

    @pl.when(new_blk & jnp.logical_not(full))
    def _():
        acc_ref[...] = jnp.zeros(acc_ref.shape, F32)

    def ffn():
        xlo, xhi = _unpack_bf16_pairs(rows_ref[...])
        half = xlo.shape[1]
        xlo = xlo.astype(BF16)
        xhi = xhi.astype(BF16)
        a = (jnp.dot(xlo, w1b_ref[0:half, :], preferred_element_type=F32)
             + jnp.dot(xhi, w1b_ref[half:2 * half, :], preferred_element_type=F32))
        b = (jnp.dot(xlo, w3b_ref[0:half, :], preferred_element_type=F32)
             + jnp.dot(xhi, w3b_ref[half:2 * half, :], preferred_element_type=F32))
        hmid = (_silu(a) * b).astype(BF16)
        return jnp.dot(hmid, w2b_ref[...], preferred_element_type=F32)

    @pl.when(full)
    def _():
        acc_ref[...] = ffn()

    @pl.when((hi > lo) & jnp.logical_not(full))
    def _():
        row = lax.broadcasted_iota(jnp.int32, (blk, 1), 0)
        keep = (row >= lo) & (row < hi)
        acc_ref[...] = acc_ref[...] + jnp.where(keep, ffn(), 0.0)

    @pl.when(last_of_blk)
    def _():
        y_ref[...] = _pack_bf16_pairs(acc_ref[...])


def _experts(wblk, wexp, wlo, whi, rows, a_rows, w1, w3, w2):
    dw = rows.shape[1]
    n_work = wblk.shape[0]
    blk = math.gcd(BLK_E, a_rows)
    _, d, ff = w1.shape
    kern = functools.partial(_expert_kernel, blk=blk, n_work=n_work)
    return pl.pallas_call(
        kern,
        grid_spec=pltpu.PrefetchScalarGridSpec(
            num_scalar_prefetch=4,
            grid=(n_work,),
            in_specs=[pl.BlockSpec((blk, dw), lambda w, wb, we, wl, wh: (wb[w], 0)),
                      pl.BlockSpec((1, d, ff), lambda w, wb, we, wl, wh: (we[w], 0, 0)),
                      pl.BlockSpec((1, d, ff), lambda w, wb, we, wl, wh: (we[w], 0, 0)),
                      pl.BlockSpec((1, ff, d), lambda w, wb, we, wl, wh: (we[w], 0, 0))],
            out_specs=pl.BlockSpec((blk, dw), lambda w, wb, we, wl, wh: (wb[w], 0)),
            scratch_shapes=[pltpu.VMEM((d, ff), BF16),
                            pltpu.VMEM((d, ff), BF16),
                            pltpu.VMEM((ff, d), BF16),
                            pltpu.VMEM((blk, d), F32)]),
        out_shape=jax.ShapeDtypeStruct((a_rows, dw), jnp.uint32),
        compiler_params=_cparams(("arbitrary",)),
        name="experts",
    )(wblk, wexp, wlo, whi, rows, w1, w3, w2)


def _work_items(counts, a_rows, blk):
    n_blk = a_rows // blk
    ends = jnp.cumsum(counts)
    starts = ends - counts
    total = ends[-1]
    pts = jnp.sort(jnp.concatenate([jnp.arange(n_blk, dtype=jnp.int32) * blk, starts[1:]]))
    nxt = jnp.concatenate([pts[1:], jnp.array([a_rows], jnp.int32)])
    blk_of = jnp.minimum(pts // blk, n_blk - 1)
    wexp = jnp.minimum(jnp.sum(ends[None, :] <= pts[:, None], axis=1), N_EXPERTS - 1).astype(jnp.int32)
    lo = pts - blk_of * blk
    hi = jnp.minimum(jnp.minimum(nxt, total), (blk_of + 1) * blk) - blk_of * blk
    hi = jnp.maximum(hi, lo)
    return blk_of.astype(jnp.int32), wexp, lo.astype(jnp.int32), hi.astype(jnp.int32)


def _combine_kernel(gsrc_ref, y_ref, x1_ref, meta_ref, mod_ref, g_ref, o_ref, sort_ref, sem, *, tm, ng):
    tile = pl.program_id(0) * pl.num_programs(1) + pl.program_id(1)
    n_tiles = pl.num_programs(0) * pl.num_programs(1)
    slot = tile % 2

    def fetch(t, s):
        def start(g, carry):
            src = gsrc_ref[t * ng + g]
            pltpu.make_async_copy(y_ref.at[src], sort_ref.at[s, g], sem.at[s]).start()
            return carry

        lax.fori_loop(0, ng, start, 0, unroll=DMA_UNROLL)

    @pl.when(tile == 0)
    def _():
        fetch(tile, slot)

    @pl.when(tile + 1 < n_tiles)
    def _():
        fetch(tile + 1, 1 - slot)

    pltpu.make_async_copy(y_ref.at[pl.ds(0, ng)], sort_ref.at[slot], sem.at[slot]).wait()

    meta = meta_ref[0]
    nrow = ng * GROUP
    lo, hi = _unpack_bf16_pairs(sort_ref[slot].reshape(nrow, sort_ref.shape[3]))
    lo = lo.astype(BF16)
    hi = hi.astype(BF16)
    col = lax.broadcasted_iota(jnp.int32, (tm, nrow), 1)
    mix = jnp.zeros((tm, nrow), F32)
    for kk in range(TOP_K):
        mix = jnp.where(col == meta[:, kk:kk + 1].astype(jnp.int32), meta[:, TOP_K + kk:TOP_K + kk + 1], mix)
    mix = mix.astype(BF16)
    moe = jnp.concatenate([jnp.dot(mix, lo, preferred_element_type=F32),
                           jnp.dot(mix, hi, preferred_element_type=F32)], axis=1)
    x2 = x1_ref[0] + mod_ref[0, 5:6, :] * moe
    ms = jnp.mean(x2 * x2, axis=-1, keepdims=True)
    o_ref[0] = x2 * lax.rsqrt(ms + EPS) * g_ref[...]


def _combine(gsrc, ybuf3, x1, meta, mod6, gf):
    bsz, s, d = x1.shape
    tm = min(TM_PROJ, s)
    ng = _local_groups(tm)
    kern = functools.partial(_combine_kernel, tm=tm, ng=ng)
    tok = lambda b, i, a: (b, i, 0)
    return pl.pallas_call(
        kern,
        grid_spec=pltpu.PrefetchScalarGridSpec(
            num_scalar_prefetch=1,
            grid=(bsz, s // tm),
            in_specs=[pl.BlockSpec(memory_space=pl.ANY),
                      pl.BlockSpec((1, tm, d), tok),
                      pl.BlockSpec((1, tm, LANES), tok),
                      pl.BlockSpec((1, 6, d), lambda b, i, a: (b, 0, 0)),
                      pl.BlockSpec((1, d), lambda b, i, a: (0, 0))],
            out_specs=pl.BlockSpec((1, tm, d), tok),
            scratch_shapes=[pltpu.VMEM((2, ng, GROUP, d // 2), jnp.uint32),
                            pltpu.SemaphoreType.DMA((2,))]),
        out_shape=jax.ShapeDtypeStruct((bsz, s, d), F32),
        compiler_params=_cparams(("arbitrary", "arbitrary")),
        name="combine",
    )(gsrc, ybuf3, x1, meta, mod6, gf.reshape(1, d))


def kernel(x, c, w_ada, b_ada, norm1_g, norm2_g, w_in, attn_lambda_q1, attn_lambda_k1, attn_lambda_q2,
           attn_lambda_k2, attn_subln_g, rel_bias_table, rec_lower_bound, rec_norm_g, w_out, w_group,
           b_group, w_expert, b_expert, w1, w3, w2, final_norm_g):
    bsz, s, d = x.shape
    depth = w_ada.shape[0]
    n_tok = bsz * s
    for l in range(depth):
        mod6 = _mod(c, w_ada[l], b_ada[l]).reshape(bsz, 6, d)
        proj, f32f = _inproj(x, mod6, norm1_g[l], w_in[l].astype(BF16))
        lq4 = jnp.stack([attn_lambda_q1[l], attn_lambda_k1[l], attn_lambda_q2[l], attn_lambda_k2[l]])
        ya = _attention(proj, lq4, attn_subln_g[l], rel_bias_table, l)
        yr = _hgrn(proj, f32f, rec_lower_bound, rec_norm_g[l], l)
        w_r = jnp.zeros((d, LANES), F32).at[:, :N_GROUPS].set(w_group[l])
        w_r = w_r.at[:, N_GROUPS:N_GROUPS + N_EXPERTS].set(w_expert[l]).astype(BF16)
        b_r = jnp.zeros((1, LANES), F32).at[0, :N_GROUPS].set(b_group[l])
        b_r = b_r.at[0, N_GROUPS:N_GROUPS + N_EXPERTS].set(b_expert[l])
        x1, h2p, meta, meta_t, cnt = _outproj(ya, yr, x, mod6, norm2_g[l], w_out[l].astype(BF16), w_r, b_r)
        tm = meta_t.shape[2]
        tcnt = cnt[:, 0, :N_EXPERTS].astype(jnp.int32)
        gdst, gsrc, counts_up = _group_tables(tcnt, tm)
        rows3 = _dispatch(gdst, meta_t, h2p.reshape(n_tok, d // 2))
        buf_rows = rows3.shape[0] * GROUP
        blk = math.gcd(BLK_E, buf_rows)
        wblk, wexp, wlo, whi = _work_items(counts_up, buf_rows, blk)
        ybuf = _experts(wblk, wexp, wlo, whi, rows3.reshape(buf_rows, d // 2), buf_rows, w1[l], w3[l], w2[l])
        gf = final_norm_g if l == depth - 1 else jnp.ones((d,), F32)
        assert depth == 1
        x = _combine(gsrc, ybuf.reshape(buf_rows // GROUP, GROUP, d // 2), x1, meta, mod6, gf)
    return x
```

```python
import functools
import math

import jax
import jax.numpy as jnp
from jax import lax
from jax.experimental import pallas as pl
from jax.experimental.pallas import tpu as pltpu

F32 = jnp.float32
BF16 = jnp.bfloat16
EPS = 1e-6

ATTN_HEADS = 4
ATTN_HEAD_DIM = 64
HEAD_W = 128
REC_HEADS = 4
REC_CHUNK = 64
NUM_BUCKETS = 32
MAX_DISTANCE = 128
N_GROUPS = 4
EXPERTS_PER_GROUP = 8
N_EXPERTS = 32
TOP_K = 2
NEG = -1e30

LANES = 128
VMEM_LIMIT = 56 * 1024 * 1024

TM_PROJ = 512
T_ATTN = 512
ATTN_COLS = 256
T_REC = 512
REC_TRI = 256
BLK_E = 512


def _cparams(sem):
    return pltpu.CompilerParams(dimension_semantics=sem, vmem_limit_bytes=VMEM_LIMIT)


def _sigmoid(x):
    return 0.5 * jnp.tanh(0.5 * x) + 0.5


def _silu(x):
    return x * _sigmoid(x)


def _mod_kernel(c_ref, w_ref, b_ref, o_ref):
    ca = _silu(c_ref[...])
    o_ref[...] = jnp.dot(ca, w_ref[...], preferred_element_type=F32,
                         precision=lax.Precision.HIGHEST) + b_ref[...]


def _mod(c, w, b):
    bsz, d = c.shape
    n = w.shape[1]
    tn = 1024
    return pl.pallas_call(
        _mod_kernel,
        grid=(n // tn,),
        in_specs=[pl.BlockSpec((bsz, d), lambda j: (0, 0)),
                  pl.BlockSpec((d, tn), lambda j: (0, j)),
                  pl.BlockSpec((1, tn), lambda j: (0, j))],
        out_specs=pl.BlockSpec((bsz, tn), lambda j: (0, j)),
        out_shape=jax.ShapeDtypeStruct((bsz, n), F32),
        compiler_params=_cparams(("arbitrary",)),
        name="mod",
    )(c, w, b.reshape(1, n))


def _norm_mod(x, g, sc, sh):
    ms = jnp.mean(x * x, axis=-1, keepdims=True)
    return (x * lax.rsqrt(ms + EPS) * g) * (1.0 + sc) + sh


def _inproj_kernel(x_ref, mod_ref, g_ref, w_ref, proj_ref, f_ref, *, f_chunk, n_chunks, cw):
    h = _norm_mod(x_ref[0], g_ref[...], mod_ref[0, 1:2, :], mod_ref[0, 0:1, :]).astype(BF16)
    for c in range(n_chunks):
        r = jnp.dot(h, w_ref[:, c * cw:(c + 1) * cw], preferred_element_type=F32)
        proj_ref[0, :, c * cw:(c + 1) * cw] = r.astype(BF16)
        if c == f_chunk:
            f_ref[0] = r


def _inproj(x, mod6, g, w_bf):
    bsz, s, d = x.shape
    n = w_bf.shape[1]
    cw = 512
    tm = min(TM_PROJ, s)
    kern = functools.partial(_inproj_kernel, f_chunk=4, n_chunks=n // cw, cw=cw)
    return pl.pallas_call(
        kern,
        grid=(bsz, s // tm),
        in_specs=[pl.BlockSpec((1, tm, d), lambda b, i: (b, i, 0)),
                  pl.BlockSpec((1, 6, d), lambda b, i: (b, 0, 0)),
                  pl.BlockSpec((1, d), lambda b, i: (0, 0)),
                  pl.BlockSpec((d, n), lambda b, i: (0, 0))],
        out_specs=[pl.BlockSpec((1, tm, n), lambda b, i: (b, i, 0)),
                   pl.BlockSpec((1, tm, cw), lambda b, i: (b, i, 0))],
        out_shape=[jax.ShapeDtypeStruct((bsz, s, n), BF16),
                   jax.ShapeDtypeStruct((bsz, s, cw), F32)],
        compiler_params=_cparams(("arbitrary", "arbitrary")),
        name="inproj",
    )(x, mod6, g.reshape(1, d), w_bf)


LOG2E = math.log2(math.e)


def _attn_kernel(lq_ref, q_ref, k_ref, v_ref, d_ref, g_ref, o_ref,
                 qs_ref, vt_ref, s_ref, m_ref, l_ref, acc_ref, *, t, lam_init):
    qi = pl.program_id(2)
    nkt = vt_ref.shape[0]
    slot = qi % 2
    blocks = [slice(c * ATTN_COLS, (c + 1) * ATTN_COLS) for c in range(2 * t // ATTN_COLS)]

    def prep(qidx, sl):
        q = q_ref[0, pl.ds(pl.multiple_of(qidx * t, t), t), :]
        lane = lax.broadcasted_iota(jnp.int32, q.shape, 1)
        qf = q.astype(F32) * (ATTN_HEAD_DIM ** -0.5 * LOG2E)
        zero = jnp.zeros_like(qf)
        qs_ref[sl, 0:t, :] = jnp.where(lane < ATTN_HEAD_DIM, qf, zero).astype(BF16)
        qs_ref[sl, t:2 * t, :] = jnp.where(lane >= ATTN_HEAD_DIM, qf, zero).astype(BF16)

    def scores(j, cs, sl):
        k = k_ref[0, pl.ds(pl.multiple_of(j * t, t), t), :]
        return lax.dot_general(k, qs_ref[sl, cs, :], (((1,), (1,)), ((), ())), preferred_element_type=F32)

    @pl.when(qi == 0)
    def _():
        for c in range(nkt):
            vt_ref[c] = v_ref[0, c * t:(c + 1) * t, :].T
        prep(0, 0)
        for cs in blocks:
            s_ref[:, cs] = scores(0, cs, 0)

    prep(jnp.minimum(qi + 1, pl.num_programs(2) - 1), 1 - slot)
    m_ref[...] = jnp.full(m_ref.shape, NEG, F32)
    l_ref[...] = jnp.zeros(l_ref.shape, F32)
    acc_ref[...] = jnp.zeros(acc_ref.shape, F32)

    def tile(j, bias_ref, j_next, next_slot=None):
        vt = vt_ref[j]
        m_all = m_ref[...]
        l_all = l_ref[...]
        ps, alphas, m_news, l_news = [], [], [], []
        for cs in blocks:
            s = s_ref[:, cs]
            if bias_ref is not None:
                s = s + bias_ref[:, cs]
            m_old = m_all[:, cs]
            m_new = jnp.maximum(m_old, jnp.max(s, axis=0, keepdims=True))
            alpha = jnp.exp2(m_old - m_new)
            p = jnp.exp2(s - m_new)
            l_news.append(alpha * l_all[:, cs] + jnp.sum(p, axis=0, keepdims=True))
            ps.append(p.astype(BF16))
            alphas.append(alpha)
            m_news.append(m_new)
            s_ref[:, cs] = scores(j_next, cs, slot if next_slot is None else next_slot)
        pvs = [jnp.dot(vt, p, preferred_element_type=F32) for p in ps]
        for cs, alpha, pv in zip(blocks, alphas, pvs):
            acc_ref[:, cs] = alpha * acc_ref[:, cs] + pv
        m_ref[...] = jnp.concatenate(m_news, axis=1)
        l_ref[...] = jnp.concatenate(l_news, axis=1)

    n_far = jnp.maximum(qi - 1, 0)

    def far_pair(jj, carry):
        tile(2 * jj, None, 2 * jj + 1)
        tile(2 * jj + 1, None, 2 * jj + 2)
        return carry

    lax.fori_loop(0, n_far // 2, far_pair, 0)

    @pl.when(n_far % 2 == 1)
    def _():
        tile(n_far - 1, None, n_far)

    @pl.when(qi >= 1)
    def _():
        tile(qi - 1, d_ref.at[0, 1], qi)

    tile(qi, d_ref.at[0, 0], 0, 1 - slot)

    lq = lq_ref[...]
    lam = (jnp.exp(jnp.sum(lq[0:1] * lq[1:2], axis=-1, keepdims=True))
           - jnp.exp(jnp.sum(lq[2:3] * lq[3:4], axis=-1, keepdims=True)) + lam_init)
    o = acc_ref[...] / l_ref[...]
    a = o[:, 0:t] - lam * o[:, t:2 * t]
    ms = jnp.mean(a * a, axis=0, keepdims=True)
    y = (a * lax.rsqrt(ms + EPS)).T * (g_ref[...] * (1.0 - lam_init))
    o_ref[0] = y.astype(o_ref.dtype)


def _t5_bucket(rel):
    n = jnp.maximum(rel, 0)
    max_exact = NUM_BUCKETS // 2
    nf = jnp.maximum(n, 1).astype(F32)
    large = max_exact + (jnp.log(nf / max_exact) / math.log(MAX_DISTANCE / max_exact)
                         * (NUM_BUCKETS - max_exact)).astype(jnp.int32)
    large = jnp.minimum(large, NUM_BUCKETS - 1)
    return jnp.where(n < max_exact, n, large)


def _bias_tiles(rel_table, t):
    assert t >= MAX_DISTANCE
    nh = rel_table.shape[1]
    rel = jnp.arange(2 * t)
    vec = (rel_table[_t5_bucket(rel)].astype(F32) - rel_table[NUM_BUCKETS - 1].astype(F32)[None, :]).T * LOG2E
    neg = jnp.full((nh, t), NEG, F32)

    def toeplitz(w):
        flat = jnp.tile(w, (1, t))[:, :t * (2 * t - 1)]
        return flat.reshape(nh, t, 2 * t - 1)[:, :, :t]

    diag = toeplitz(jnp.concatenate([vec[:, :t], neg], axis=1))
    prev = toeplitz(jnp.concatenate([vec[:, t:], vec[:, :t]], axis=1))
    both = jnp.stack([diag, prev], axis=1)
    return jnp.concatenate([both, both], axis=-1)


def _attention(proj, lq4, subln_g, rel_table, layer):
    bsz, s, _ = proj.shape
    t = min(T_ATTN, s)
    nq = s // t
    lam_init = 0.8 - 0.6 * math.exp(-0.3 * layer)
    dt = _bias_tiles(rel_table, t)
    hq, hk, hv = 0, ATTN_HEADS, 2 * ATTN_HEADS
    kern = functools.partial(_attn_kernel, t=t, lam_init=lam_init)
    return pl.pallas_call(
        kern,
        grid=(bsz, ATTN_HEADS, nq),
        in_specs=[pl.BlockSpec((4, ATTN_HEAD_DIM), lambda b, h, i: (0, 0)),
                  pl.BlockSpec((1, s, HEAD_W), lambda b, h, i: (b, 0, hq + h)),
                  pl.BlockSpec((1, s, HEAD_W), lambda b, h, i: (b, 0, hk + h)),
                  pl.BlockSpec((1, s, HEAD_W), lambda b, h, i: (b, 0, hv + h)),
                  pl.BlockSpec((1, 2, t, 2 * t), lambda b, h, i: (h, 0, 0, 0)),
                  pl.BlockSpec((1, HEAD_W), lambda b, h, i: (0, 0))],
        out_specs=pl.BlockSpec((1, t, HEAD_W), lambda b, h, i: (b, i, h)),
        out_shape=jax.ShapeDtypeStruct((bsz, s, ATTN_HEADS * HEAD_W), BF16),
        scratch_shapes=[pltpu.VMEM((2, 2 * t, HEAD_W), BF16),
                        pltpu.VMEM((nq, HEAD_W, t), BF16),
                        pltpu.VMEM((t, 2 * t), F32),
                        pltpu.VMEM((1, 2 * t), F32),
                        pltpu.VMEM((1, 2 * t), F32),
                        pltpu.VMEM((HEAD_W, 2 * t), F32)],
        compiler_params=_cparams(("arbitrary", "arbitrary", "arbitrary")),
        name="attn",
    )(lq4, proj, proj, proj, dt, subln_g.reshape(1, HEAD_W))


def _hgrn_kernel(lb_ref, tri_ref, q_ref, f_ref, i_ref, g_ref, ng_ref, o_ref, st_ref, *, tr, layer):
    c = REC_CHUNK
    nch = tr // c

    @pl.when(pl.program_id(2) == 0)
    def _():
        st_ref[...] = jnp.zeros(st_ref.shape, F32)

    lbr = lb_ref[...]
    e = jnp.exp(lbr - jnp.max(lbr, axis=0, keepdims=True))
    lb = jnp.sum(e[0:layer + 1], axis=0, keepdims=True) / jnp.sum(e, axis=0, keepdims=True)

    fr = f_ref[0]
    sg = _sigmoid(fr)
    lf = jnp.log2(lb + (1.0 - lb) * sg)
    kf = (1.0 - lb) * (1.0 - sg)
    qf = _silu(q_ref[0].astype(F32))
    vb = i_ref[0]

    row = lax.broadcasted_iota(jnp.int32, (tr, LANES), 0)

    lf_hi = lf.astype(BF16)
    lf_lo = (lf - lf_hi.astype(F32)).astype(BF16)
    lf2 = jnp.concatenate([lf_hi, lf_lo], axis=1)
    tw = tri_ref.shape[0]
    parts = []
    for r0 in range(0, tr, tw):
        cs2 = jnp.dot(tri_ref[...], lf2[r0:r0 + tw], preferred_element_type=F32)
        parts.append(cs2[:, :LANES] + cs2[:, LANES:])
    b = jnp.concatenate(parts, axis=0)

    kf3 = kf.reshape(tr // 8, 8, LANES)
    b3 = b.reshape(tr // 8, 8, LANES)
    diags = [jnp.sum(qf * kf, axis=-1, keepdims=True)]
    for dlt in range(1, 8):
        kd = pltpu.roll(kf3, dlt, axis=1).reshape(tr, LANES)
        bd = pltpu.roll(b3, dlt, axis=1).reshape(tr, LANES)
        term = qf * kd * jnp.exp2(b - bd)
        diags.append(jnp.sum(term, axis=-1, keepdims=True))
    lv_q, lv_k = [], []
    for w in (8, 16, 32):
        bend = jnp.broadcast_to(b.reshape(tr // w, w, LANES)[:, w - 1:w, :],
                                (tr // w, w, LANES)).reshape(tr, LANES)
        right = (row // w) % 2 == 1
        pend = pltpu.roll(bend, w, axis=0)
        lv_q.append(jnp.where(right, qf * jnp.exp2(jnp.minimum(b - pend, 0.0)), 0.0).astype(BF16))
        lv_k.append(jnp.where(right, 0.0, kf * jnp.exp2(jnp.minimum(bend - b, 0.0))).astype(BF16))

    blast = jnp.broadcast_to(b.reshape(nch, c, LANES)[:, c - 1:c, :], (nch, c, LANES)).reshape(tr, LANES)
    qdec = (qf * jnp.exp2(b)).astype(BF16)
    kdec = (kf * jnp.exp2(blast - b)).astype(BF16)
    dec = jnp.exp2(blast)

    ri = lax.broadcasted_iota(jnp.int32, (c, c), 0)
    ci = lax.broadcasted_iota(jnp.int32, (c, c), 1)
    nt = (((1,), (1,)), ((), ()))
    sls = [slice(ch * c, (ch + 1) * c) for ch in range(nch)]
    sws = [[lax.dot_general(lv_q[li][sl], lv_k[li][sl], nt, preferred_element_type=F32) for li in range(3)]
           for sl in sls]
    upds = [lax.dot_general(vb[sl], kdec[sl], (((0,), (0,)), ((), ())), preferred_element_type=F32)
            for sl in sls]
    sames = [ri // (2 * w) == ci // (2 * w) for w in (8, 16, 32)]
    dsel = [(ci == ri - dlt) & (ri % 8 >= dlt) for dlt in range(8)]
    a_bf = []
    for ch, sl in enumerate(sls):
        a = jnp.zeros((c, c), F32)
        for li in range(3):
            a = a + jnp.where(sames[li], sws[ch][li], 0.0)
        for dlt in range(8):
            a = a + jnp.where(dsel[dlt], diags[dlt][sl], 0.0)
        a_bf.append(a.astype(BF16))
    st = st_ref[...]
    sts = []
    for ch in range(nch):
        sts.append(st.astype(BF16))
        st = dec[ch * c:ch * c + 1, :] * st + upds[ch]
    st_ref[...] = st
    outs = [lax.dot_general(qdec[sl], sts[ch], nt, preferred_element_type=F32)
            + jnp.dot(a_bf[ch], vb[sl], preferred_element_type=F32) for ch, sl in enumerate(sls)]

    o = jnp.concatenate(outs, axis=0)
    ms = jnp.mean(o * o, axis=-1, keepdims=True)
    y = (o * lax.rsqrt(ms + EPS) * ng_ref[...]) * _silu(g_ref[0].astype(F32))
    o_ref[0] = y.astype(o_ref.dtype)


def _hgrn(proj, f32f, lb_raw, norm_g, layer):
    bsz, s, _ = proj.shape
    tr = min(T_REC, s)
    nl = lb_raw.shape[0]
    cq, ci, cg = 12, 20, 24
    kern = functools.partial(_hgrn_kernel, tr=tr, layer=layer)
    tw = min(REC_TRI, tr)
    r = jnp.arange(tw)
    tri = ((r[:, None] // REC_CHUNK == r[None, :] // REC_CHUNK) & (r[None, :] <= r[:, None])).astype(BF16)
    return pl.pallas_call(
        kern,
        grid=(bsz, REC_HEADS, s // tr),
        in_specs=[pl.BlockSpec((nl, HEAD_W), lambda b, h, i: (0, h)),
                  pl.BlockSpec((tw, tw), lambda b, h, i: (0, 0)),
                  pl.BlockSpec((1, tr, HEAD_W), lambda b, h, i: (b, i, cq + h)),
                  pl.BlockSpec((1, tr, HEAD_W), lambda b, h, i: (b, i, h)),
                  pl.BlockSpec((1, tr, HEAD_W), lambda b, h, i: (b, i, ci + h)),
                  pl.BlockSpec((1, tr, HEAD_W), lambda b, h, i: (b, i, cg + h)),
                  pl.BlockSpec((1, HEAD_W), lambda b, h, i: (0, h))],
        out_specs=pl.BlockSpec((1, tr, HEAD_W), lambda b, h, i: (b, i, h)),
        out_shape=jax.ShapeDtypeStruct((bsz, s, REC_HEADS * HEAD_W), BF16),
        scratch_shapes=[pltpu.VMEM((HEAD_W, HEAD_W), F32)],
        compiler_params=_cparams(("arbitrary", "arbitrary", "arbitrary")),
        name="hgrn",
    )(lb_raw, tri, proj, f32f, proj, proj, norm_g.reshape(1, -1))


def _pack_bf16_pairs(x):
    n = x.shape[1] // 2
    lo = pltpu.bitcast(x[:, :n].astype(BF16).astype(F32), jnp.uint32)
    hi = pltpu.bitcast(x[:, n:].astype(BF16).astype(F32), jnp.uint32)
    return (hi & jnp.uint32(0xFFFF0000)) | (lo >> 16)


def _join_bf16_pairs(lo, hi):
    return pltpu.bitcast(hi, jnp.uint32) | (pltpu.bitcast(lo, jnp.uint32) >> 16)


def _unpack_bf16_pairs(w):
    lo = pltpu.bitcast(w << 16, F32)
    hi = pltpu.bitcast(w & jnp.uint32(0xFFFF0000), F32)
    return lo, hi


def _outproj_kernel(ya_ref, yr_ref, x_ref, mod_ref, g_ref, wo_ref, wr_ref, br_ref,
                    x1_ref, h2_ref, meta_ref, metat_ref, cnt_ref, *, tm):
    half = ya_ref.shape[2]
    mix = (jnp.dot(ya_ref[0], wo_ref[0:half, :], preferred_element_type=F32)
           + jnp.dot(yr_ref[0], wo_ref[half:2 * half, :], preferred_element_type=F32))
    x1 = x_ref[0] + mod_ref[0, 2:3, :] * mix
    x1_ref[0] = x1
    h2 = _norm_mod(x1, g_ref[...], mod_ref[0, 4:5, :], mod_ref[0, 3:4, :])
    h2_ref[0] = _pack_bf16_pairs(h2)

    logits = jnp.dot(h2.astype(BF16), wr_ref[...], preferred_element_type=F32) + br_ref[...]
    lane = lax.broadcasted_iota(jnp.int32, logits.shape, 1)
    lane_f = lane.astype(F32)

    def first_lane(mask):
        return jnp.min(jnp.where(mask, lane_f, float(LANES)), axis=-1, keepdims=True).astype(jnp.int32)

    is_g = lane < N_GROUPS
    gl = jnp.where(is_g, logits, NEG)
    gmax = jnp.max(gl, axis=-1, keepdims=True)
    gsum = jnp.sum(jnp.exp(gl - gmax), axis=-1, keepdims=True)
    g_gate = 1.0 / gsum
    g_idx = first_lane(is_g & (gl == gmax))
    e_lo = N_GROUPS + g_idx * EXPERTS_PER_GROUP
    in_grp = (lane >= e_lo) & (lane < e_lo + EXPERTS_PER_GROUP)
    el = jnp.where(in_grp, logits, NEG)
    m1 = jnp.max(el, axis=-1, keepdims=True)
    i1 = first_lane(in_grp & (el == m1))
    el2 = jnp.where(lane == i1, NEG, el)
    m2 = jnp.max(el2, axis=-1, keepdims=True)
    i2 = first_lane(in_grp & (el2 == m2))
    r = jnp.exp(m2 - m1)
    w1 = g_gate / (1.0 + r)
    w2 = g_gate * r / (1.0 + r)
    e1 = i1 - N_GROUPS
    e2 = i2 - N_GROUPS

    oh1 = (lane == e1).astype(F32)
    oh2 = (lane == e2).astype(F32)
    oh = oh1 + oh2
    ri = lax.broadcasted_iota(jnp.int32, (tm, tm), 0)
    ci = lax.broadcasted_iota(jnp.int32, (tm, tm), 1)
    tri = (ci < ri).astype(BF16)
    rank = jnp.dot(tri, oh.astype(BF16), preferred_element_type=F32)
    cnt = jnp.sum(oh, axis=0, keepdims=True)
    cnt_up = jnp.floor((cnt + (GROUP - 1.0)) * (1.0 / GROUP)) * GROUP
    li = lax.broadcasted_iota(jnp.int32, (LANES, LANES), 0)
    lj = lax.broadcasted_iota(jnp.int32, (LANES, LANES), 1)
    before = (li < lj).astype(BF16)
    lstart = jnp.dot(jnp.broadcast_to(cnt_up, (8, LANES)).astype(BF16), before,
                     preferred_element_type=F32)[0:1, :]
    pos = rank + lstart
    p1 = jnp.sum(pos * oh1, axis=-1, keepdims=True)
    p2 = jnp.sum(pos * oh2, axis=-1, keepdims=True)
    vals = (p1, p2, w1, w2)
    meta = jnp.zeros(logits.shape, F32)
    for i, v in enumerate(vals):
        meta = jnp.where(lane == i, v, meta)
    meta_ref[0] = meta
    metat_ref[0] = meta.T[0:8, :]
    cnt_ref[0] = jnp.broadcast_to(cnt, (8, LANES))


def _outproj(ya, yr, x, mod6, g2n, wo_bf, wr_bf, br):
    bsz, s, d = x.shape
    tm = min(TM_PROJ, s)
    half = ya.shape[2]
    kern = functools.partial(_outproj_kernel, tm=tm)
    tok = lambda b, i: (b, i, 0)
    nt = s // tm
    tile = lambda b, i: (b * nt + i, 0, 0)
    return pl.pallas_call(
        kern,
        grid=(bsz, nt),
        in_specs=[pl.BlockSpec((1, tm, half), tok),
                  pl.BlockSpec((1, tm, half), tok),
                  pl.BlockSpec((1, tm, d), tok),
                  pl.BlockSpec((1, 6, d), lambda b, i: (b, 0, 0)),
                  pl.BlockSpec((1, d), lambda b, i: (0, 0)),
                  pl.BlockSpec((2 * half, d), lambda b, i: (0, 0)),
                  pl.BlockSpec((d, LANES), lambda b, i: (0, 0)),
                  pl.BlockSpec((1, LANES), lambda b, i: (0, 0))],
        out_specs=[pl.BlockSpec((1, tm, d), tok),
                   pl.BlockSpec((1, tm, d // 2), tok),
                   pl.BlockSpec((1, tm, LANES), tok),
                   pl.BlockSpec((1, 8, tm), tile),
                   pl.BlockSpec((1, 8, LANES), tile)],
        out_shape=[jax.ShapeDtypeStruct((bsz, s, d), F32),
                   jax.ShapeDtypeStruct((bsz, s, d // 2), jnp.uint32),
                   jax.ShapeDtypeStruct((bsz, s, LANES), F32),
                   jax.ShapeDtypeStruct((bsz * nt, 8, tm), F32),
                   jax.ShapeDtypeStruct((bsz * nt, 8, LANES), F32)],
        compiler_params=_cparams(("arbitrary", "arbitrary")),
        name="outproj",
    )(ya, yr, x, mod6, g2n.reshape(1, d), wo_bf, wr_bf, br)


GROUP = 8
DMA_UNROLL = 8


def _local_groups(tm):
    need = (tm * TOP_K + N_EXPERTS * (GROUP - 1)) // GROUP
    return -(-need // DMA_UNROLL) * DMA_UNROLL


def _dispatch_kernel(gdst_ref, metat_ref, h_ref, rows_ref, sort_ref, sem, *, tm, ng):
    tile = pl.program_id(0)
    last = pl.num_programs(0) - 1
    slot = tile % 2

    def drain(s):
        pltpu.make_async_copy(sort_ref.at[s], rows_ref.at[pl.ds(0, ng)], sem.at[s]).wait()

    @pl.when(tile >= 2)
    def _():
        drain(slot)

    p1 = metat_ref[0, 0:1, :].astype(jnp.int32)
    p2 = metat_ref[0, 1:2, :].astype(jnp.int32)
    r = lax.broadcasted_iota(jnp.int32, (ng * GROUP, tm), 0)
    perm = ((r == p1) | (r == p2)).astype(BF16)
    lo, hi = _unpack_bf16_pairs(h_ref[...])
    slo = jnp.dot(perm, lo.astype(BF16), preferred_element_type=F32)
    shi = jnp.dot(perm, hi.astype(BF16), preferred_element_type=F32)
    sort_ref[slot] = _join_bf16_pairs(slo, shi).reshape(sort_ref.shape[1:])

    def start(g, carry):
        dst = gdst_ref[tile * ng + g]
        pltpu.make_async_copy(sort_ref.at[slot, g], rows_ref.at[dst], sem.at[slot]).start()
        return carry

    lax.fori_loop(0, ng, start, 0, unroll=DMA_UNROLL)

    @pl.when(tile == last)
    def _():
        @pl.when(tile >= 1)
        def _():
            drain(1 - slot)

        drain(slot)


def _dispatch(gdst, meta_t, h2p):
    n, dw = h2p.shape
    tm = meta_t.shape[2]
    ng = _local_groups(tm)
    kern = functools.partial(_dispatch_kernel, tm=tm, ng=ng)
    return pl.pallas_call(
        kern,
        grid_spec=pltpu.PrefetchScalarGridSpec(
            num_scalar_prefetch=1,
            grid=(n // tm,),
            in_specs=[pl.BlockSpec((1, 8, tm), lambda i, a: (i, 0, 0)),
                      pl.BlockSpec((tm, dw), lambda i, a: (i, 0))],
            out_specs=pl.BlockSpec(memory_space=pl.ANY),
            scratch_shapes=[pltpu.VMEM((2, ng, GROUP, dw), jnp.uint32),
                            pltpu.SemaphoreType.DMA((2,))]),
        out_shape=jax.ShapeDtypeStruct((n // tm * ng, GROUP, dw), jnp.uint32),
        compiler_params=_cparams(("arbitrary",)),
        name="dispatch",
    )(gdst, meta_t, h2p)


def _group_tables(tcnt, tm):
    ng = _local_groups(tm)
    g8 = (tcnt + (GROUP - 1)) // GROUP
    gcount = jnp.sum(g8, axis=0)
    gstart = jnp.cumsum(gcount) - gcount
    gtile = gstart[None, :] + jnp.cumsum(g8, axis=0) - g8
    lend = jnp.cumsum(g8, axis=1)
    lstart = lend - g8
    g = jnp.arange(ng, dtype=jnp.int32)
    exp_of = jnp.sum(lend[:, None, :] <= g[None, :, None], axis=-1)
    onehot = exp_of[..., None] == jnp.arange(N_EXPERTS, dtype=jnp.int32)
    base = jnp.sum(jnp.where(onehot, (gtile - lstart)[:, None, :], 0), axis=-1)
    n_used = lend[:, -1:]
    used = g[None, :] < n_used
    n_spare = ng - n_used
    spare0 = jnp.sum(gcount) + jnp.cumsum(n_spare, axis=0) - n_spare
    gdst = jnp.where(used, base + g[None, :], spare0 + g[None, :] - n_used).astype(jnp.int32).reshape(-1)
    gsrc = jnp.where(used, base + g[None, :], 0).astype(jnp.int32).reshape(-1)
    return gdst, gsrc, gcount * GROUP


def _expert_kernel(wblk_ref, wexp_ref, wlo_ref, whi_ref, rows_ref, w1_ref, w3_ref, w2_ref, y_ref,
                   w1b_ref, w3b_ref, w2b_ref, acc_ref, *, blk, n_work):
    w = pl.program_id(0)
    prev = jnp.maximum(w - 1, 0)
    nxt = jnp.minimum(w + 1, n_work - 1)
    new_exp = (w == 0) | (wexp_ref[w] != wexp_ref[prev])
    new_blk = (w == 0) | (wblk_ref[w] != wblk_ref[prev])
    last_of_blk = (w == n_work - 1) | (wblk_ref[w] != wblk_ref[nxt])
    lo = wlo_ref[w]
    hi = whi_ref[w]

    @pl.when(new_exp)
    def _():
        w1b_ref[...] = w1_ref[0].astype(BF16)
        w3b_ref[...] = w3_ref[0].astype(BF16)
        w2b_ref[...] = w2_ref[0].astype(BF16)

    full = (lo == 0) & (hi == blk)

    @pl.when(new_blk & jnp.logical_not(full))
    def _():
        acc_ref[...] = jnp.zeros(acc_ref.shape, F32)

    def ffn():
        xlo, xhi = _unpack_bf16_pairs(rows_ref[...])
        half = xlo.shape[1]
        xlo = xlo.astype(BF16)
        xhi = xhi.astype(BF16)
        a = (jnp.dot(xlo, w1b_ref[0:half, :], preferred_element_type=F32)
             + jnp.dot(xhi, w1b_ref[half:2 * half, :], preferred_element_type=F32))
        b = (jnp.dot(xlo, w3b_ref[0:half, :], preferred_element_type=F32)
             + jnp.dot(xhi, w3b_ref[half:2 * half, :], preferred_element_type=F32))
        hmid = (_silu(a) * b).astype(BF16)
        return jnp.dot(hmid, w2b_ref[...], preferred_element_type=F32)

    @pl.when(full)
    def _():
        acc_ref[...] = ffn()

    @pl.when((hi > lo) & jnp.logical_not(full))
    def _():
        row = lax.broadcasted_iota(jnp.int32, (blk, 1), 0)
        keep = (row >= lo) & (row < hi)
        acc_ref[...] = acc_ref[...] + jnp.where(keep, ffn(), 0.0)

    @pl.when(last_of_blk)
    def _():
        y_ref[...] = _pack_bf16_pairs(acc_ref[...])


def _experts(wblk, wexp, wlo, whi, rows, a_rows, w1, w3, w2):
    dw = rows.shape[1]
    n_work = wblk.shape[0]
    blk = math.gcd(BLK_E, a_rows)
    _, d, ff = w1.shape
    kern = functools.partial(_expert_kernel, blk=blk, n_work=n_work)
    return pl.pallas_call(
        kern,
        grid_spec=pltpu.PrefetchScalarGridSpec(
            num_scalar_prefetch=4,
            grid=(n_work,),
            in_specs=[pl.BlockSpec((blk, dw), lambda w, wb, we, wl, wh: (wb[w], 0)),
                      pl.BlockSpec((1, d, ff), lambda w, wb, we, wl, wh: (we[w], 0, 0)),
                      pl.BlockSpec((1, d, ff), lambda w, wb, we, wl, wh: (we[w], 0, 0)),
                      pl.BlockSpec((1, ff, d), lambda w, wb, we, wl, wh: (we[w], 0, 0))],
            out_specs=pl.BlockSpec((blk, dw), lambda w, wb, we, wl, wh: (wb[w], 0)),
            scratch_shapes=[pltpu.VMEM((d, ff), BF16),
                            pltpu.VMEM((d, ff), BF16),
                            pltpu.VMEM((ff, d), BF16),
                            pltpu.VMEM((blk, d), F32)]),
        out_shape=jax.ShapeDtypeStruct((a_rows, dw), jnp.uint32),
        compiler_params=_cparams(("arbitrary",)),
        name="experts",
    )(wblk, wexp, wlo, whi, rows, w1, w3, w2)


def _work_items(counts, a_rows, blk):
    n_blk = a_rows // blk
    ends = jnp.cumsum(counts)
    starts = ends - counts
    total = ends[-1]
    pts = jnp.sort(jnp.concatenate([jnp.arange(n_blk, dtype=jnp.int32) * blk, starts[1:]]))
    nxt = jnp.concatenate([pts[1:], jnp.array([a_rows], jnp.int32)])
    blk_of = jnp.minimum(pts // blk, n_blk - 1)
    wexp = jnp.minimum(jnp.sum(ends[None, :] <= pts[:, None], axis=1), N_EXPERTS - 1).astype(jnp.int32)
    lo = pts - blk_of * blk
    hi = jnp.minimum(jnp.minimum(nxt, total), (blk_of + 1) * blk) - blk_of * blk
    hi = jnp.maximum(hi, lo)
    return blk_of.astype(jnp.int32), wexp, lo.astype(jnp.int32), hi.astype(jnp.int32)


def _combine_kernel(gsrc_ref, y_ref, x1_ref, meta_ref, mod_ref, g_ref, o_ref, sort_ref, sem, *, tm, ng):
    tile = pl.program_id(0) * pl.num_programs(1) + pl.program_id(1)
    n_tiles = pl.num_programs(0) * pl.num_programs(1)
    slot = tile % 2

    def fetch(t, s):
        def start(g, carry):
            src = gsrc_ref[t * ng + g]
            pltpu.make_async_copy(y_ref.at[src], sort_ref.at[s, g], sem.at[s]).start()
            return carry

        lax.fori_loop(0, ng, start, 0, unroll=DMA_UNROLL)

    @pl.when(tile == 0)
    def _():
        fetch(tile, slot)

    @pl.when(tile + 1 < n_tiles)
    def _():
        fetch(tile + 1, 1 - slot)

    pltpu.make_async_copy(y_ref.at[pl.ds(0, ng)], sort_ref.at[slot], sem.at[slot]).wait()

    meta = meta_ref[0]
    nrow = ng * GROUP
    lo, hi = _unpack_bf16_pairs(sort_ref[slot].reshape(nrow, sort_ref.shape[3]))
    lo = lo.astype(BF16)
    hi = hi.astype(BF16)
    col = lax.broadcasted_iota(jnp.int32, (tm, nrow), 1)
    mix = jnp.zeros((tm, nrow), F32)
    for kk in range(TOP_K):
        mix = jnp.where(col == meta[:, kk:kk + 1].astype(jnp.int32), meta[:, TOP_K + kk:TOP_K + kk + 1], mix)
    mix = mix.astype(BF16)
    moe = jnp.concatenate([jnp.dot(mix, lo, preferred_element_type=F32),
                           jnp.dot(mix, hi, preferred_element_type=F32)], axis=1)
    x2 = x1_ref[0] + mod_ref[0, 5:6, :] * moe
    ms = jnp.mean(x2 * x2, axis=-1, keepdims=True)
    o_ref[0] = x2 * lax.rsqrt(ms + EPS) * g_ref[...]


def _combine(gsrc, ybuf3, x1, meta, mod6, gf):
    bsz, s, d = x1.shape
    tm = min(TM_PROJ, s)
    ng = _local_groups(tm)
    kern = functools.partial(_combine_kernel, tm=tm, ng=ng)
    tok = lambda b, i, a: (b, i, 0)
    return pl.pallas_call(
        kern,
        grid_spec=pltpu.PrefetchScalarGridSpec(
            num_scalar_prefetch=1,
            grid=(bsz, s // tm),
            in_specs=[pl.BlockSpec(memory_space=pl.ANY),
                      pl.BlockSpec((1, tm, d), tok),
                      pl.BlockSpec((1, tm, LANES), tok),
                      pl.BlockSpec((1, 6, d), lambda b, i, a: (b, 0, 0)),
                      pl.BlockSpec((1, d), lambda b, i, a: (0, 0))],
            out_specs=pl.BlockSpec((1, tm, d), tok),
            scratch_shapes=[pltpu.VMEM((2, ng, GROUP, d // 2), jnp.uint32),
                            pltpu.SemaphoreType.DMA((2,))]),
        out_shape=jax.ShapeDtypeStruct((bsz, s, d), F32),
        compiler_params=_cparams(("arbitrary", "arbitrary")),
        name="combine",
    )(gsrc, ybuf3, x1, meta, mod6, gf.reshape(1, d))


def kernel(x, c, w_ada, b_ada, norm1_g, norm2_g, w_in, attn_lambda_q1, attn_lambda_k1, attn_lambda_q2,
           attn_lambda_k2, attn_subln_g, rel_bias_table, rec_lower_bound, rec_norm_g, w_out, w_group,
           b_group, w_expert, b_expert, w1, w3, w2, final_norm_g):
    bsz, s, d = x.shape
    depth = w_ada.shape[0]
    n_tok = bsz * s
    for l in range(depth):
        mod6 = _mod(c, w_ada[l], b_ada[l]).reshape(bsz, 6, d)
        proj, f32f = _inproj(x, mod6, norm1_g[l], w_in[l].astype(BF16))
        lq4 = jnp.stack([attn_lambda_q1[l], attn_lambda_k1[l], attn_lambda_q2[l], attn_lambda_k2[l]])
        ya = _attention(proj, lq4, attn_subln_g[l], rel_bias_table, l)
        yr = _hgrn(proj, f32f, rec_lower_bound, rec_norm_g[l], l)
        w_r = jnp.zeros((d, LANES), F32).at[:, :N_GROUPS].set(w_group[l])
        w_r = w_r.at[:, N_GROUPS:N_GROUPS + N_EXPERTS].set(w_expert[l]).astype(BF16)
        b_r = jnp.zeros((1, LANES), F32).at[0, :N_GROUPS].set(b_group[l])
        b_r = b_r.at[0, N_GROUPS:N_GROUPS + N_EXPERTS].set(b_expert[l])
        x1, h2p, meta, meta_t, cnt = _outproj(ya, yr, x, mod6, norm2_g[l], w_out[l].astype(BF16), w_r, b_r)
        tm = meta_t.shape[2]
        tcnt = cnt[:, 0, :N_EXPERTS].astype(jnp.int32)
        gdst, gsrc, counts_up = _group_tables(tcnt, tm)
        rows3 = _dispatch(gdst, meta_t, h2p.reshape(n_tok, d // 2))
        buf_rows = rows3.shape[0] * GROUP
        blk = math.gcd(BLK_E, buf_rows)
        wblk, wexp, wlo, whi = _work_items(counts_up, buf_rows, blk)
        ybuf = _experts(wblk, wexp, wlo, whi, rows3.reshape(buf_rows, d // 2), buf_rows, w1[l], w3[l], w2[l])
        gf = final_norm_g if l == depth - 1 else jnp.ones((d,), F32)
        assert depth == 1
        x = _combine(gsrc, ybuf.reshape(buf_rows // GROUP, GROUP, d // 2), x1, meta, mod6, gf)
    return x
```

```python
import functools
import math

import jax
import jax.numpy as jnp
from jax import lax
from jax.experimental import pallas as pl
from jax.experimental.pallas import tpu as pltpu

F32 = jnp.float32
BF16 = jnp.bfloat16
EPS = 1e-6

ATTN_HEADS = 4
ATTN_HEAD_DIM = 64
HEAD_W = 128
REC_HEADS = 4
REC_CHUNK = 64
NUM_BUCKETS = 32
MAX_DISTANCE = 128
N_GROUPS = 4
EXPERTS_PER_GROUP = 8
N_EXPERTS = 32
TOP_K = 2
NEG = -1e30

LANES = 128
VMEM_LIMIT = 56 * 1024 * 1024

TM_PROJ = 512
OUT_ROWS = 128
T_ATTN = 512
ATTN_COLS = 512
T_REC = 512
REC_TRI = 256
BLK_E = 512


def _cparams(sem):
    return pltpu.CompilerParams(dimension_semantics=sem, vmem_limit_bytes=VMEM_LIMIT)


def _sigmoid(x):
    return 0.5 * jnp.tanh(0.5 * x) + 0.5


def _silu(x):
    return x * _sigmoid(x)


def _mod_kernel(c_ref, w_ref, b_ref, o_ref):
    ca = _silu(c_ref[...])
    o_ref[...] = jnp.dot(ca, w_ref[...], preferred_element_type=F32,
                         precision=lax.Precision.HIGHEST) + b_ref[...]


def _mod(c, w, b):
    bsz, d = c.shape
    n = w.shape[1]
    tn = 1024
    return pl.pallas_call(
        _mod_kernel,
        grid=(n // tn,),
        in_specs=[pl.BlockSpec((bsz, d), lambda j: (0, 0)),
                  pl.BlockSpec((d, tn), lambda j: (0, j)),
                  pl.BlockSpec((1, tn), lambda j: (0, j))],
        out_specs=pl.BlockSpec((bsz, tn), lambda j: (0, j)),
        out_shape=jax.ShapeDtypeStruct((bsz, n), F32),
        compiler_params=_cparams(("arbitrary",)),
        name="mod",
    )(c, w, b.reshape(1, n))


def _norm_mod(x, g, sc, sh):
    ms = jnp.mean(x * x, axis=-1, keepdims=True)
    return (x * lax.rsqrt(ms + EPS) * g) * (1.0 + sc) + sh


def _inproj_kernel(x_ref, mod_ref, g_ref, w_ref, proj_ref, f_ref, *, f_chunk, n_chunks, cw):
    h = _norm_mod(x_ref[0], g_ref[...], mod_ref[0, 1:2, :], mod_ref[0, 0:1, :]).astype(BF16)
    for c in range(n_chunks):
        r = jnp.dot(h, w_ref[:, c * cw:(c + 1) * cw], preferred_element_type=F32)
        proj_ref[0, :, c * cw:(c + 1) * cw] = r.astype(BF16)
        if c == f_chunk:
            f_ref[0] = r


def _inproj(x, mod6, g, w_bf):
    bsz, s, d = x.shape
    n = w_bf.shape[1]
    cw = 512
    tm = min(TM_PROJ, s)
    kern = functools.partial(_inproj_kernel, f_chunk=4, n_chunks=n // cw, cw=cw)
    return pl.pallas_call(
        kern,
        grid=(bsz, s // tm),
        in_specs=[pl.BlockSpec((1, tm, d), lambda b, i: (b, i, 0)),
                  pl.BlockSpec((1, 6, d), lambda b, i: (b, 0, 0)),
                  pl.BlockSpec((1, d), lambda b, i: (0, 0)),
                  pl.BlockSpec((d, n), lambda b, i: (0, 0))],
        out_specs=[pl.BlockSpec((1, tm, n), lambda b, i: (b, i, 0)),
                   pl.BlockSpec((1, tm, cw), lambda b, i: (b, i, 0))],
        out_shape=[jax.ShapeDtypeStruct((bsz, s, n), BF16),
                   jax.ShapeDtypeStruct((bsz, s, cw), F32)],
        compiler_params=_cparams(("arbitrary", "arbitrary")),
        name="inproj",
    )(x, mod6, g.reshape(1, d), w_bf)


LOG2E = math.log2(math.e)


def _attn_kernel(lq_ref, q_ref, k_ref, v_ref, d_ref, g_ref, o_ref,
                 qs_ref, vt_ref, s_ref, m_ref, l_ref, acc_ref, *, t, lam_init):
    qi = pl.program_id(2)
    nkt = vt_ref.shape[0]
    slot = qi % 2
    blocks = [slice(c * ATTN_COLS, (c + 1) * ATTN_COLS) for c in range(2 * t // ATTN_COLS)]

    def prep(qidx, sl):
        q = q_ref[0, pl.ds(pl.multiple_of(qidx * t, t), t), :]
        lane = lax.broadcasted_iota(jnp.int32, q.shape, 1)
        qf = q.astype(F32) * (ATTN_HEAD_DIM ** -0.5 * LOG2E)
        zero = jnp.zeros_like(qf)
        qs_ref[sl, 0:t, :] = jnp.where(lane < ATTN_HEAD_DIM, qf, zero).astype(BF16)
        qs_ref[sl, t:2 * t, :] = jnp.where(lane >= ATTN_HEAD_DIM, qf, zero).astype(BF16)

    def scores(j, cs, sl):
        k = k_ref[0, pl.ds(pl.multiple_of(j * t, t), t), :]
        return lax.dot_general(k, qs_ref[sl, cs, :], (((1,), (1,)), ((), ())), preferred_element_type=F32)

    @pl.when(qi == 0)
    def _():
        for c in range(nkt):
            vt_ref[c] = v_ref[0, c * t:(c + 1) * t, :].T
        prep(0, 0)
        for cs in blocks:
            s_ref[:, cs] = scores(0, cs, 0)

    prep(jnp.minimum(qi + 1, pl.num_programs(2) - 1), 1 - slot)
    m_ref[...] = jnp.full(m_ref.shape, NEG, F32)
    l_ref[...] = jnp.zeros(l_ref.shape, F32)
    acc_ref[...] = jnp.zeros(acc_ref.shape, F32)

    def tile(j, bias_ref, j_next, next_slot=None):
        vt = vt_ref[j]
        m_all = m_ref[...]
        l_all = l_ref[...]
        ps, alphas, m_news, l_news = [], [], [], []
        for cs in blocks:
            s = s_ref[:, cs]
            if bias_ref is not None:
                s = s + bias_ref[:, cs]
            m_old = m_all[:, cs]
            m_new = jnp.maximum(m_old, jnp.max(s, axis=0, keepdims=True))
            alpha = jnp.exp2(m_old - m_new)
            p = jnp.exp2(s - m_new)
            l_news.append(alpha * l_all[:, cs] + jnp.sum(p, axis=0, keepdims=True))
            ps.append(p.astype(BF16))
            alphas.append(alpha)
            m_news.append(m_new)
            s_ref[:, cs] = scores(j_next, cs, slot if next_slot is None else next_slot)
        pvs = [jnp.dot(vt, p, preferred_element_type=F32) for p in ps]
        for cs, alpha, pv in zip(blocks, alphas, pvs):
            acc_ref[:, cs] = alpha * acc_ref[:, cs] + pv
        m_ref[...] = jnp.concatenate(m_news, axis=1)
        l_ref[...] = jnp.concatenate(l_news, axis=1)

    n_far = jnp.maximum(qi - 1, 0)

    def far_pair(jj, carry):
        tile(2 * jj, None, 2 * jj + 1)
        tile(2 * jj + 1, None, 2 * jj + 2)
        return carry

    lax.fori_loop(0, n_far // 2, far_pair, 0)

    @pl.when(n_far % 2 == 1)
    def _():
        tile(n_far - 1, None, n_far)

    @pl.when(qi >= 1)
    def _():
        tile(qi - 1, d_ref.at[0, 1], qi)

    tile(qi, d_ref.at[0, 0], 0, 1 - slot)

    lq = lq_ref[...]
    lam = (jnp.exp(jnp.sum(lq[0:1] * lq[1:2], axis=-1, keepdims=True))
           - jnp.exp(jnp.sum(lq[2:3] * lq[3:4], axis=-1, keepdims=True)) + lam_init)
    o = acc_ref[...] / l_ref[...]
    a = o[:, 0:t] - lam * o[:, t:2 * t]
    ms = jnp.mean(a * a, axis=0, keepdims=True)
    y = (a * lax.rsqrt(ms + EPS)).T * (g_ref[...] * (1.0 - lam_init))
    o_ref[0] = y.astype(o_ref.dtype)


def _t5_bucket(rel):
    n = jnp.maximum(rel, 0)
    max_exact = NUM_BUCKETS // 2
    nf = jnp.maximum(n, 1).astype(F32)
    large = max_exact + (jnp.log(nf / max_exact) / math.log(MAX_DISTANCE / max_exact)
                         * (NUM_BUCKETS - max_exact)).astype(jnp.int32)
    large = jnp.minimum(large, NUM_BUCKETS - 1)
    return jnp.where(n < max_exact, n, large)


def _bias_tiles(rel_table, t):
    assert t >= MAX_DISTANCE
    nh = rel_table.shape[1]
    rel = jnp.arange(2 * t)
    vec = (rel_table[_t5_bucket(rel)].astype(F32) - rel_table[NUM_BUCKETS - 1].astype(F32)[None, :]).T * LOG2E
    neg = jnp.full((nh, t), NEG, F32)

    def toeplitz(w):
        flat = jnp.tile(w, (1, t))[:, :t * (2 * t - 1)]
        return flat.reshape(nh, t, 2 * t - 1)[:, :, :t]

    diag = toeplitz(jnp.concatenate([vec[:, :t], neg], axis=1))
    prev = toeplitz(jnp.concatenate([vec[:, t:], vec[:, :t]], axis=1))
    both = jnp.stack([diag, prev], axis=1)
    return jnp.concatenate([both, both], axis=-1)


def _attention(proj, lq4, subln_g, rel_table, layer):
    bsz, s, _ = proj.shape
    t = min(T_ATTN, s)
    nq = s // t
    lam_init = 0.8 - 0.6 * math.exp(-0.3 * layer)
    dt = _bias_tiles(rel_table, t)
    hq, hk, hv = 0, ATTN_HEADS, 2 * ATTN_HEADS
    kern = functools.partial(_attn_kernel, t=t, lam_init=lam_init)
    return pl.pallas_call(
        kern,
        grid=(bsz, ATTN_HEADS, nq),
        in_specs=[pl.BlockSpec((4, ATTN_HEAD_DIM), lambda b, h, i: (0, 0)),
                  pl.BlockSpec((1, s, HEAD_W), lambda b, h, i: (b, 0, hq + h)),
                  pl.BlockSpec((1, s, HEAD_W), lambda b, h, i: (b, 0, hk + h)),
                  pl.BlockSpec((1, s, HEAD_W), lambda b, h, i: (b, 0, hv + h)),
                  pl.BlockSpec((1, 2, t, 2 * t), lambda b, h, i: (h, 0, 0, 0)),
                  pl.BlockSpec((1, HEAD_W), lambda b, h, i: (0, 0))],
        out_specs=pl.BlockSpec((1, t, HEAD_W), lambda b, h, i: (b, i, h)),
        out_shape=jax.ShapeDtypeStruct((bsz, s, ATTN_HEADS * HEAD_W), BF16),
        scratch_shapes=[pltpu.VMEM((2, 2 * t, HEAD_W), BF16),
                        pltpu.VMEM((nq, HEAD_W, t), BF16),
                        pltpu.VMEM((t, 2 * t), F32),
                        pltpu.VMEM((1, 2 * t), F32),
                        pltpu.VMEM((1, 2 * t), F32),
                        pltpu.VMEM((HEAD_W, 2 * t), F32)],
        compiler_params=_cparams(("arbitrary", "arbitrary", "arbitrary")),
        name="attn",
    )(lq4, proj, proj, proj, dt, subln_g.reshape(1, HEAD_W))


def _hgrn_kernel(lb_ref, tri_ref, q_ref, f_ref, i_ref, g_ref, ng_ref, o_ref, st_ref, *, tr, layer):
    c = REC_CHUNK
    nch = tr // c

    @pl.when(pl.program_id(2) == 0)
    def _():
        st_ref[...] = jnp.zeros(st_ref.shape, F32)

    lbr = lb_ref[...]
    e = jnp.exp(lbr - jnp.max(lbr, axis=0, keepdims=True))
    lb = jnp.sum(e[0:layer + 1], axis=0, keepdims=True) / jnp.sum(e, axis=0, keepdims=True)

    fr = f_ref[0]
    sg = _sigmoid(fr)
    lf = jnp.log2(lb + (1.0 - lb) * sg)
    kf = (1.0 - lb) * (1.0 - sg)
    qf = _silu(q_ref[0].astype(F32))
    vb = i_ref[0]

    row = lax.broadcasted_iota(jnp.int32, (tr, LANES), 0)

    lf_hi = lf.astype(BF16)
    lf_lo = (lf - lf_hi.astype(F32)).astype(BF16)
    lf2 = jnp.concatenate([lf_hi, lf_lo], axis=1)
    tw = tri_ref.shape[0]
    parts = []
    for r0 in range(0, tr, tw):
        cs2 = jnp.dot(tri_ref[...], lf2[r0:r0 + tw], preferred_element_type=F32)
        parts.append(cs2[:, :LANES] + cs2[:, LANES:])
    b = jnp.concatenate(parts, axis=0)

    kf3 = kf.reshape(tr // 8, 8, LANES)
    b3 = b.reshape(tr // 8, 8, LANES)
    diags = [jnp.sum(qf * kf, axis=-1, keepdims=True)]
    for dlt in range(1, 8):
        kd = pltpu.roll(kf3, dlt, axis=1).reshape(tr, LANES)
        bd = pltpu.roll(b3, dlt, axis=1).reshape(tr, LANES)
        term = qf * kd * jnp.exp2(b - bd)
        diags.append(jnp.sum(term, axis=-1, keepdims=True))
    lv_q, lv_k = [], []
    for w in (8, 16, 32):
        bend = jnp.broadcast_to(b.reshape(tr // w, w, LANES)[:, w - 1:w, :],
                                (tr // w, w, LANES)).reshape(tr, LANES)
        right = (row // w) % 2 == 1
        pend = pltpu.roll(bend, w, axis=0)
        lv_q.append(jnp.where(right, qf * jnp.exp2(jnp.minimum(b - pend, 0.0)), 0.0).astype(BF16))
        lv_k.append(jnp.where(right, 0.0, kf * jnp.exp2(jnp.minimum(bend - b, 0.0))).astype(BF16))

    blast = jnp.broadcast_to(b.reshape(nch, c, LANES)[:, c - 1:c, :], (nch, c, LANES)).reshape(tr, LANES)
    qdec = (qf * jnp.exp2(b)).astype(BF16)
    kdec = (kf * jnp.exp2(blast - b)).astype(BF16)
    dec = jnp.exp2(blast)

    ri = lax.broadcasted_iota(jnp.int32, (c, c), 0)
    ci = lax.broadcasted_iota(jnp.int32, (c, c), 1)
    nt = (((1,), (1,)), ((), ()))
    sls = [slice(ch * c, (ch + 1) * c) for ch in range(nch)]
    sws = [[lax.dot_general(lv_q[li][sl], lv_k[li][sl], nt, preferred_element_type=F32) for li in range(3)]
           for sl in sls]
    upds = [lax.dot_general(vb[sl], kdec[sl], (((0,), (0,)), ((), ())), preferred_element_type=F32)
            for sl in sls]
    sames = [ri // (2 * w) == ci // (2 * w) for w in (8, 16, 32)]
    dsel = [(ci == ri - dlt) & (ri % 8 >= dlt) for dlt in range(8)]
    a_bf = []
    for ch, sl in enumerate(sls):
        a = jnp.zeros((c, c), F32)
        for li in range(3):
            a = a + jnp.where(sames[li], sws[ch][li], 0.0)
        for dlt in range(8):
            a = a + jnp.where(dsel[dlt], diags[dlt][sl], 0.0)
        a_bf.append(a.astype(BF16))
    st = st_ref[...]
    sts = []
    for ch in range(nch):
        sts.append(st.astype(BF16))
        st = dec[ch * c:ch * c + 1, :] * st + upds[ch]
    st_ref[...] = st
    outs = [lax.dot_general(qdec[sl], sts[ch], nt, preferred_element_type=F32)
            + jnp.dot(a_bf[ch], vb[sl], preferred_element_type=F32) for ch, sl in enumerate(sls)]

    o = jnp.concatenate(outs, axis=0)
    ms = jnp.mean(o * o, axis=-1, keepdims=True)
    y = (o * lax.rsqrt(ms + EPS) * ng_ref[...]) * _silu(g_ref[0].astype(F32))
    o_ref[0] = y.astype(o_ref.dtype)


def _hgrn(proj, f32f, lb_raw, norm_g, layer):
    bsz, s, _ = proj.shape
    tr = min(T_REC, s)
    nl = lb_raw.shape[0]
    cq, ci, cg = 12, 20, 24
    kern = functools.partial(_hgrn_kernel, tr=tr, layer=layer)
    tw = min(REC_TRI, tr)
    r = jnp.arange(tw)
    tri = ((r[:, None] // REC_CHUNK == r[None, :] // REC_CHUNK) & (r[None, :] <= r[:, None])).astype(BF16)
    return pl.pallas_call(
        kern,
        grid=(bsz, REC_HEADS, s // tr),
        in_specs=[pl.BlockSpec((nl, HEAD_W), lambda b, h, i: (0, h)),
                  pl.BlockSpec((tw, tw), lambda b, h, i: (0, 0)),
                  pl.BlockSpec((1, tr, HEAD_W), lambda b, h, i: (b, i, cq + h)),
                  pl.BlockSpec((1, tr, HEAD_W), lambda b, h, i: (b, i, h)),
                  pl.BlockSpec((1, tr, HEAD_W), lambda b, h, i: (b, i, ci + h)),
                  pl.BlockSpec((1, tr, HEAD_W), lambda b, h, i: (b, i, cg + h)),
                  pl.BlockSpec((1, HEAD_W), lambda b, h, i: (0, h))],
        out_specs=pl.BlockSpec((1, tr, HEAD_W), lambda b, h, i: (b, i, h)),
        out_shape=jax.ShapeDtypeStruct((bsz, s, REC_HEADS * HEAD_W), BF16),
        scratch_shapes=[pltpu.VMEM((HEAD_W, HEAD_W), F32)],
        compiler_params=_cparams(("arbitrary", "arbitrary", "arbitrary")),
        name="hgrn",
    )(lb_raw, tri, proj, f32f, proj, proj, norm_g.reshape(1, -1))


def _pack_bf16_pairs(x):
    n = x.shape[1] // 2
    lo = pltpu.bitcast(x[:, :n].astype(BF16).astype(F32), jnp.uint32)
    hi = pltpu.bitcast(x[:, n:].astype(BF16).astype(F32), jnp.uint32)
    return (hi & jnp.uint32(0xFFFF0000)) | (lo >> 16)


def _join_bf16_pairs(lo, hi):
    return pltpu.bitcast(hi, jnp.uint32) | (pltpu.bitcast(lo, jnp.uint32) >> 16)


def _unpack_bf16_pairs(w):
    lo = pltpu.bitcast(w << 16, F32)
    hi = pltpu.bitcast(w & jnp.uint32(0xFFFF0000), F32)
    return lo, hi


def _outproj_kernel(ya_ref, yr_ref, x_ref, mod_ref, g_ref, wo_ref, wr_ref, br_ref,
                    x1_ref, h2_ref, meta_ref, metat_ref, cnt_ref, *, tm):
    half = ya_ref.shape[2]
    subs = [slice(r0, r0 + OUT_ROWS) for r0 in range(0, tm, OUT_ROWS)]
    lane = lax.broadcasted_iota(jnp.int32, (tm, LANES), 1)
    lane_f = lane.astype(F32)

    def first_lane(mask):
        return jnp.min(jnp.where(mask, lane_f, float(LANES)), axis=-1, keepdims=True).astype(jnp.int32)

    def route(logits):
        is_g = lane < N_GROUPS
        gl = jnp.where(is_g, logits, NEG)
        gmax = jnp.max(gl, axis=-1, keepdims=True)
        gsum = jnp.sum(jnp.exp(gl - gmax), axis=-1, keepdims=True)
        g_gate = 1.0 / gsum
        g_idx = first_lane(is_g & (gl == gmax))
        e_lo = N_GROUPS + g_idx * EXPERTS_PER_GROUP
        in_grp = (lane >= e_lo) & (lane < e_lo + EXPERTS_PER_GROUP)
        el = jnp.where(in_grp, logits, NEG)
        m1 = jnp.max(el, axis=-1, keepdims=True)
        i1 = first_lane(in_grp & (el == m1))
        el2 = jnp.where(lane == i1, NEG, el)
        m2 = jnp.max(el2, axis=-1, keepdims=True)
        i2 = first_lane(in_grp & (el2 == m2))
        r = jnp.exp(m2 - m1)
        return i1 - N_GROUPS, i2 - N_GROUPS, g_gate / (1.0 + r), g_gate * r / (1.0 + r)

    mixes = [jnp.dot(ya_ref[0, rs, :], wo_ref[0:half, :], preferred_element_type=F32)
             + jnp.dot(yr_ref[0, rs, :], wo_ref[half:2 * half, :], preferred_element_type=F32) for rs in subs]
    logit_l = []
    for rs, mix in zip(subs, mixes):
        x1 = x_ref[0, rs, :] + mod_ref[0, 2:3, :] * mix
        x1_ref[0, rs, :] = x1
        h2 = _norm_mod(x1, g_ref[...], mod_ref[0, 4:5, :], mod_ref[0, 3:4, :])
        h2_ref[0, rs, :] = _pack_bf16_pairs(h2)
        logit_l.append(jnp.dot(h2.astype(BF16), wr_ref[...], preferred_element_type=F32) + br_ref[...])
    e1, e2, w1, w2 = route(jnp.concatenate(logit_l, axis=0))

    oh1 = (lane == e1).astype(F32)
    oh2 = (lane == e2).astype(F32)
    oh = oh1 + oh2
    ri = lax.broadcasted_iota(jnp.int32, (tm, tm), 0)
    ci = lax.broadcasted_iota(jnp.int32, (tm, tm), 1)
    tri = (ci < ri).astype(BF16)
    rank = jnp.dot(tri, oh.astype(BF16), preferred_element_type=F32)
    cnt = jnp.sum(oh, axis=0, keepdims=True)
    cnt_up = jnp.floor((cnt + (GROUP - 1.0)) * (1.0 / GROUP)) * GROUP
    li = lax.broadcasted_iota(jnp.int32, (LANES, LANES), 0)
    lj = lax.broadcasted_iota(jnp.int32, (LANES, LANES), 1)
    before = (li < lj).astype(BF16)
    lstart = jnp.dot(jnp.broadcast_to(cnt_up, (8, LANES)).astype(BF16), before,
                     preferred_element_type=F32)[0:1, :]
    pos = rank + lstart
    p1 = jnp.sum(pos * oh1, axis=-1, keepdims=True)
    p2 = jnp.sum(pos * oh2, axis=-1, keepdims=True)
    vals = (p1, p2, w1, w2)
    meta = jnp.zeros((tm, LANES), F32)
    for i, v in enumerate(vals):
        meta = jnp.where(lane == i, v, meta)
    meta_ref[0] = meta
    metat_ref[0] = meta.T[0:8, :]
    cnt_ref[0] = jnp.broadcast_to(cnt, (8, LANES))


def _outproj(ya, yr, x, mod6, g2n, wo_bf, wr_bf, br):
    bsz, s, d = x.shape
    tm = min(TM_PROJ, s)
    half = ya.shape[2]
    kern = functools.partial(_outproj_kernel, tm=tm)
    tok = lambda b, i: (b, i, 0)
    nt = s // tm
    tile = lambda b, i: (b * nt + i, 0, 0)
    return pl.pallas_call(
        kern,
        grid=(bsz, nt),
        in_specs=[pl.BlockSpec((1, tm, half), tok),
                  pl.BlockSpec((1, tm, half), tok),
                  pl.BlockSpec((1, tm, d), tok),
                  pl.BlockSpec((1, 6, d), lambda b, i: (b, 0, 0)),
                  pl.BlockSpec((1, d), lambda b, i: (0, 0)),
                  pl.BlockSpec((2 * half, d), lambda b, i: (0, 0)),
                  pl.BlockSpec((d, LANES), lambda b, i: (0, 0)),
                  pl.BlockSpec((1, LANES), lambda b, i: (0, 0))],
        out_specs=[pl.BlockSpec((1, tm, d), tok),
                   pl.BlockSpec((1, tm, d // 2), tok),
                   pl.BlockSpec((1, tm, LANES), tok),
                   pl.BlockSpec((1, 8, tm), tile),
                   pl.BlockSpec((1, 8, LANES), tile)],
        out_shape=[jax.ShapeDtypeStruct((bsz, s, d), F32),
                   jax.ShapeDtypeStruct((bsz, s, d // 2), jnp.uint32),
                   jax.ShapeDtypeStruct((bsz, s, LANES), F32),
                   jax.ShapeDtypeStruct((bsz * nt, 8, tm), F32),
                   jax.ShapeDtypeStruct((bsz * nt, 8, LANES), F32)],
        compiler_params=_cparams(("arbitrary", "arbitrary")),
        name="outproj",
    )(ya, yr, x, mod6, g2n.reshape(1, d), wo_bf, wr_bf, br)


GROUP = 8
DMA_UNROLL = 8


def _local_groups(tm):
    need = (tm * TOP_K + N_EXPERTS * (GROUP - 1)) // GROUP
    return -(-need // DMA_UNROLL) * DMA_UNROLL


def _dispatch_kernel(gdst_ref, metat_ref, h_ref, rows_ref, sort_ref, sem, *, tm, ng):
    tile = pl.program_id(0)
    last = pl.num_programs(0) - 1
    slot = tile % 2

    def drain(s):
        pltpu.make_async_copy(sort_ref.at[s], rows_ref.at[pl.ds(0, ng)], sem.at[s]).wait()

    @pl.when(tile >= 2)
    def _():
        drain(slot)

    p1 = metat_ref[0, 0:1, :].astype(jnp.int32)
    p2 = metat_ref[0, 1:2, :].astype(jnp.int32)
    r = lax.broadcasted_iota(jnp.int32, (ng * GROUP, tm), 0)
    perm = ((r == p1) | (r == p2)).astype(BF16)
    lo, hi = _unpack_bf16_pairs(h_ref[...])
    slo = jnp.dot(perm, lo.astype(BF16), preferred_element_type=F32)
    shi = jnp.dot(perm, hi.astype(BF16), preferred_element_type=F32)
    sort_ref[slot] = _join_bf16_pairs(slo, shi).reshape(sort_ref.shape[1:])

    def start(g, carry):
        dst = gdst_ref[tile * ng + g]
        pltpu.make_async_copy(sort_ref.at[slot, g], rows_ref.at[dst], sem.at[slot]).start()
        return carry

    lax.fori_loop(0, ng, start, 0, unroll=DMA_UNROLL)

    @pl.when(tile == last)
    def _():
        @pl.when(tile >= 1)
        def _():
            drain(1 - slot)

        drain(slot)


def _dispatch(gdst, meta_t, h2p):
    n, dw = h2p.shape
    tm = meta_t.shape[2]
    ng = _local_groups(tm)
    kern = functools.partial(_dispatch_kernel, tm=tm, ng=ng)
    return pl.pallas_call(
        kern,
        grid_spec=pltpu.PrefetchScalarGridSpec(
            num_scalar_prefetch=1,
            grid=(n // tm,),
            in_specs=[pl.BlockSpec((1, 8, tm), lambda i, a: (i, 0, 0)),
                      pl.BlockSpec((tm, dw), lambda i, a: (i, 0))],
            out_specs=pl.BlockSpec(memory_space=pl.ANY),
            scratch_shapes=[pltpu.VMEM((2, ng, GROUP, dw), jnp.uint32),
                            pltpu.SemaphoreType.DMA((2,))]),
        out_shape=jax.ShapeDtypeStruct((n // tm * ng, GROUP, dw), jnp.uint32),
        compiler_params=_cparams(("arbitrary",)),
        name="dispatch",
    )(gdst, meta_t, h2p)


def _group_tables(tcnt, tm):
    ng = _local_groups(tm)
    g8 = (tcnt + (GROUP - 1)) // GROUP
    gcount = jnp.sum(g8, axis=0)
    gstart = jnp.cumsum(gcount) - gcount
    gtile = gstart[None, :] + jnp.cumsum(g8, axis=0) - g8
    lend = jnp.cumsum(g8, axis=1)
    lstart = lend - g8
    g = jnp.arange(ng, dtype=jnp.int32)
    exp_of = jnp.sum(lend[:, None, :] <= g[None, :, None], axis=-1)
    onehot = exp_of[..., None] == jnp.arange(N_EXPERTS, dtype=jnp.int32)
    base = jnp.sum(jnp.where(onehot, (gtile - lstart)[:, None, :], 0), axis=-1)
    n_used = lend[:, -1:]
    used = g[None, :] < n_used
    n_spare = ng - n_used
    spare0 = jnp.sum(gcount) + jnp.cumsum(n_spare, axis=0) - n_spare
    gdst = jnp.where(used, base + g[None, :], spare0 + g[None, :] - n_used).astype(jnp.int32).reshape(-1)
    gsrc = jnp.where(used, base + g[None, :], 0).astype(jnp.int32).reshape(-1)
    return gdst, gsrc, gcount * GROUP


def _expert_kernel(wblk_ref, wexp_ref, wlo_ref, whi_ref, rows_ref, w1_ref, w3_ref, w2_ref, y_ref,
                   w1b_ref, w3b_ref, w2b_ref, acc_ref, *, blk, n_work):
    w = pl.program_id(0)
    prev = jnp.maximum(w - 1, 0)
    nxt = jnp.minimum(w + 1, n_work - 1)
    new_exp = (w == 0) | (wexp_ref[w] != wexp_ref[prev])
    new_blk = (w == 0) | (wblk_ref[w] != wblk_ref[prev])
    last_of_blk = (w == n_work - 1) | (wblk_ref[w] != wblk_ref[nxt])
    lo = wlo_ref[w]
    hi = whi_ref[w]

    @pl.when(new_exp)
    def _():
        w1b_ref[...] = w1_ref[0].astype(BF16)
        w3b_ref[...] = w3_ref[0].astype(BF16)
        w2b_ref[...] = w2_ref[0].astype(BF16)

    full = (lo == 0) & (hi == blk)

    @pl.when(new_blk & jnp.logical_not(full))
    def _():
        acc_ref[...] = jnp.zeros(acc_ref.shape, F32)

    def ffn():
        xlo, xhi = _unpack_bf16_pairs(rows_ref[...])
        half = xlo.shape[1]
        xlo = xlo.astype(BF16)
        xhi = xhi.astype(BF16)
        a = (jnp.dot(xlo, w1b_ref[0:half, :], preferred_element_type=F32)
             + jnp.dot(xhi, w1b_ref[half:2 * half, :], preferred_element_type=F32))
        b = (jnp.dot(xlo, w3b_ref[0:half, :], preferred_element_type=F32)
             + jnp.dot(xhi, w3b_ref[half:2 * half, :], preferred_element_type=F32))
        hmid = (_silu(a) * b).astype(BF16)
        return jnp.dot(hmid, w2b_ref[...], preferred_element_type=F32)

    @pl.when(full)
    def _():
        acc_ref[...] = ffn()

    @pl.when((hi > lo) & jnp.logical_not(full))
    def _():
        row = lax.broadcasted_iota(jnp.int32, (blk, 1), 0)
        keep = (row >= lo) & (row < hi)
        acc_ref[...] = acc_ref[...] + jnp.where(keep, ffn(), 0.0)

    @pl.when(last_of_blk)
    def _():
        y_ref[...] = _pack_bf16_pairs(acc_ref[...])


def _experts(wblk, wexp, wlo, whi, rows, a_rows, w1, w3, w2):
    dw = rows.shape[1]
    n_work = wblk.shape[0]
    blk = math.gcd(BLK_E, a_rows)
    _, d, ff = w1.shape
    kern = functools.partial(_expert_kernel, blk=blk, n_work=n_work)
    return pl.pallas_call(
        kern,
        grid_spec=pltpu.PrefetchScalarGridSpec(
            num_scalar_prefetch=4,
            grid=(n_work,),
            in_specs=[pl.BlockSpec((blk, dw), lambda w, wb, we, wl, wh: (wb[w], 0)),
                      pl.BlockSpec((1, d, ff), lambda w, wb, we, wl, wh: (we[w], 0, 0)),
                      pl.BlockSpec((1, d, ff), lambda w, wb, we, wl, wh: (we[w], 0, 0)),
                      pl.BlockSpec((1, ff, d), lambda w, wb, we, wl, wh: (we[w], 0, 0))],
            out_specs=pl.BlockSpec((blk, dw), lambda w, wb, we, wl, wh: (wb[w], 0)),
            scratch_shapes=[pltpu.VMEM((d, ff), BF16),
                            pltpu.VMEM((d, ff), BF16),
                            pltpu.VMEM((ff, d), BF16),
                            pltpu.VMEM((blk, d), F32)]),
        out_shape=jax.ShapeDtypeStruct((a_rows, dw), jnp.uint32),
        compiler_params=_cparams(("arbitrary",)),
        name="experts",
    )(wblk, wexp, wlo, whi, rows, w1, w3, w2)


def _work_items(counts, a_rows, blk):
    n_blk = a_rows // blk
    ends = jnp.cumsum(counts)
    starts = ends - counts
    total = ends[-1]
    pts = jnp.sort(jnp.concatenate([jnp.arange(n_blk, dtype=jnp.int32) * blk, starts[1:]]))
    nxt = jnp.concatenate([pts[1:], jnp.array([a_rows], jnp.int32)])
    blk_of = jnp.minimum(pts // blk, n_blk - 1)
    wexp = jnp.minimum(jnp.sum(ends[None, :] <= pts[:, None], axis=1), N_EXPERTS - 1).astype(jnp.int32)
    lo = pts - blk_of * blk
    hi = jnp.minimum(jnp.minimum(nxt, total), (blk_of + 1) * blk) - blk_of * blk
    hi = jnp.maximum(hi, lo)
    return blk_of.astype(jnp.int32), wexp, lo.astype(jnp.int32), hi.astype(jnp.int32)


def _combine_kernel(gsrc_ref, y_ref, x1_ref, meta_ref, mod_ref, g_ref, o_ref, sort_ref, sem, *, tm, ng):
    tile = pl.program_id(0) * pl.num_programs(1) + pl.program_id(1)
    n_tiles = pl.num_programs(0) * pl.num_programs(1)
    slot = tile % 2

    def fetch(t, s):
        def start(g, carry):
            src = gsrc_ref[t * ng + g]
            pltpu.make_async_copy(y_ref.at[src], sort_ref.at[s, g], sem.at[s]).start()
            return carry

        lax.fori_loop(0, ng, start, 0, unroll=DMA_UNROLL)

    @pl.when(tile == 0)
    def _():
        fetch(tile, slot)

    @pl.when(tile + 1 < n_tiles)
    def _():
        fetch(tile + 1, 1 - slot)

    pltpu.make_async_copy(y_ref.at[pl.ds(0, ng)], sort_ref.at[slot], sem.at[slot]).wait()

    meta = meta_ref[0]
    nrow = ng * GROUP
    lo, hi = _unpack_bf16_pairs(sort_ref[slot].reshape(nrow, sort_ref.shape[3]))
    lo = lo.astype(BF16)
    hi = hi.astype(BF16)
    col = lax.broadcasted_iota(jnp.int32, (tm, nrow), 1)
    mix = jnp.zeros((tm, nrow), F32)
    for kk in range(TOP_K):
        mix = jnp.where(col == meta[:, kk:kk + 1].astype(jnp.int32), meta[:, TOP_K + kk:TOP_K + kk + 1], mix)
    mix = mix.astype(BF16)
    moe = jnp.concatenate([jnp.dot(mix, lo, preferred_element_type=F32),
                           jnp.dot(mix, hi, preferred_element_type=F32)], axis=1)
    x2 = x1_ref[0] + mod_ref[0, 5:6, :] * moe
    ms = jnp.mean(x2 * x2, axis=-1, keepdims=True)
    o_ref[0] = x2 * lax.rsqrt(ms + EPS) * g_ref[...]


def _combine(gsrc, ybuf3, x1, meta, mod6, gf):
    bsz, s, d = x1.shape
    tm = min(TM_PROJ, s)
    ng = _local_groups(tm)
    kern = functools.partial(_combine_kernel, tm=tm, ng=ng)
    tok = lambda b, i, a: (b, i, 0)
    return pl.pallas_call(
        kern,
        grid_spec=pltpu.PrefetchScalarGridSpec(
            num_scalar_prefetch=1,
            grid=(bsz, s // tm),
            in_specs=[pl.BlockSpec(memory_space=pl.ANY),
                      pl.BlockSpec((1, tm, d), tok),
                      pl.BlockSpec((1, tm, LANES), tok),
                      pl.BlockSpec((1, 6, d), lambda b, i, a: (b, 0, 0)),
                      pl.BlockSpec((1, d), lambda b, i, a: (0, 0))],
            out_specs=pl.BlockSpec((1, tm, d), tok),
            scratch_shapes=[pltpu.VMEM((2, ng, GROUP, d // 2), jnp.uint32),
                            pltpu.SemaphoreType.DMA((2,))]),
        out_shape=jax.ShapeDtypeStruct((bsz, s, d), F32),
        compiler_params=_cparams(("arbitrary", "arbitrary")),
        name="combine",
    )(gsrc, ybuf3, x1, meta, mod6, gf.reshape(1, d))


def kernel(x, c, w_ada, b_ada, norm1_g, norm2_g, w_in, attn_lambda_q1, attn_lambda_k1, attn_lambda_q2,
           attn_lambda_k2, attn_subln_g, rel_bias_table, rec_lower_bound, rec_norm_g, w_out, w_group,
           b_group, w_expert, b_expert, w1, w3, w2, final_norm_g):
    bsz, s, d = x.shape
    depth = w_ada.shape[0]
    n_tok = bsz * s
    for l in range(depth):
        mod6 = _mod(c, w_ada[l], b_ada[l]).reshape(bsz, 6, d)
        proj, f32f = _inproj(x, mod6, norm1_g[l], w_in[l].astype(BF16))
        lq4 = jnp.stack([attn_lambda_q1[l], attn_lambda_k1[l], attn_lambda_q2[l], attn_lambda_k2[l]])
        ya = _attention(proj, lq4, attn_subln_g[l], rel_bias_table, l)
        yr = _hgrn(proj, f32f, rec_lower_bound, rec_norm_g[l], l)
        w_r = jnp.zeros((d, LANES), F32).at[:, :N_GROUPS].set(w_group[l])
        w_r = w_r.at[:, N_GROUPS:N_GROUPS + N_EXPERTS].set(w_expert[l]).astype(BF16)
        b_r = jnp.zeros((1, LANES), F32).at[0, :N_GROUPS].set(b_group[l])
        b_r = b_r.at[0, N_GROUPS:N_GROUPS + N_EXPERTS].set(b_expert[l])
        x1, h2p, meta, meta_t, cnt = _outproj(ya, yr, x, mod6, norm2_g[l], w_out[l].astype(BF16), w_r, b_r)
        tm = meta_t.shape[2]
        tcnt = cnt[:, 0, :N_EXPERTS].astype(jnp.int32)
        gdst, gsrc, counts_up = _group_tables(tcnt, tm)
        rows3 = _dispatch(gdst, meta_t, h2p.reshape(n_tok, d // 2))
        buf_rows = rows3.shape[0] * GROUP
        blk = math.gcd(BLK_E, buf_rows)
        wblk, wexp, wlo, whi = _work_items(counts_up, buf_rows, blk)
        ybuf = _experts(wblk, wexp, wlo, whi, rows3.reshape(buf_rows, d // 2), buf_rows, w1[l], w3[l], w2[l])
        gf = final_norm_g if l == depth - 1 else jnp.ones((d,), F32)
        assert depth == 1
        x = _combine(gsrc, ybuf.reshape(buf_rows // GROUP, GROUP, d // 2), x1, meta, mod6, gf)
    return x
```

```python
import functools
import math

import jax
import jax.numpy as jnp
from jax import lax
from jax.experimental import pallas as pl
from jax.experimental.pallas import tpu as pltpu

F32 = jnp.float32
BF16 = jnp.bfloat16
EPS = 1e-6

ATTN_HEADS = 4
ATTN_HEAD_DIM = 64
HEAD_W = 128
REC_HEADS = 4
REC_CHUNK = 64
NUM_BUCKETS = 32
MAX_DISTANCE = 128
N_GROUPS = 4
EXPERTS_PER_GROUP = 8
N_EXPERTS = 32
TOP_K = 2
NEG = -1e30

LANES = 128
VMEM_LIMIT = 56 * 1024 * 1024

TM_PROJ = 512
OUT_ROWS = 128
T_ATTN = 512
ATTN_COLS = 1024
T_REC = 512
REC_TRI = 256
BLK_E = 512


def _cparams(sem):
    return pltpu.CompilerParams(dimension_semantics=sem, vmem_limit_bytes=VMEM_LIMIT)


def _sigmoid(x):
    return 0.5 * jnp.tanh(0.5 * x) + 0.5


def _silu(x):
    return x * _sigmoid(x)


def _mod_kernel(c_ref, w_ref, b_ref, o_ref):
    ca = _silu(c_ref[...])
    o_ref[...] = jnp.dot(ca, w_ref[...], preferred_element_type=F32,
                         precision=lax.Precision.HIGHEST) + b_ref[...]


def _mod(c, w, b):
    bsz, d = c.shape
    n = w.shape[1]
    tn = 1024
    return pl.pallas_call(
        _mod_kernel,
        grid=(n // tn,),
        in_specs=[pl.BlockSpec((bsz, d), lambda j: (0, 0)),
                  pl.BlockSpec((d, tn), lambda j: (0, j)),
                  pl.BlockSpec((1, tn), lambda j: (0, j))],
        out_specs=pl.BlockSpec((bsz, tn), lambda j: (0, j)),
        out_shape=jax.ShapeDtypeStruct((bsz, n), F32),
        compiler_params=_cparams(("arbitrary",)),
        name="mod",
    )(c, w, b.reshape(1, n))


def _norm_mod(x, g, sc, sh):
    ms = jnp.mean(x * x, axis=-1, keepdims=True)
    return (x * lax.rsqrt(ms + EPS) * g) * (1.0 + sc) + sh


def _inproj_kernel(x_ref, mod_ref, g_ref, w_ref, proj_ref, f_ref, *, f_chunk, n_chunks, cw):
    h = _norm_mod(x_ref[0], g_ref[...], mod_ref[0, 1:2, :], mod_ref[0, 0:1, :]).astype(BF16)
    for c in range(n_chunks):
        r = jnp.dot(h, w_ref[:, c * cw:(c + 1) * cw], preferred_element_type=F32)
        proj_ref[0, :, c * cw:(c + 1) * cw] = r.astype(BF16)
        if c == f_chunk:
            f_ref[0] = r


def _inproj(x, mod6, g, w_bf):
    bsz, s, d = x.shape
    n = w_bf.shape[1]
    cw = 512
    tm = min(TM_PROJ, s)
    kern = functools.partial(_inproj_kernel, f_chunk=4, n_chunks=n // cw, cw=cw)
    return pl.pallas_call(
        kern,
        grid=(bsz, s // tm),
        in_specs=[pl.BlockSpec((1, tm, d), lambda b, i: (b, i, 0)),
                  pl.BlockSpec((1, 6, d), lambda b, i: (b, 0, 0)),
                  pl.BlockSpec((1, d), lambda b, i: (0, 0)),
                  pl.BlockSpec((d, n), lambda b, i: (0, 0))],
        out_specs=[pl.BlockSpec((1, tm, n), lambda b, i: (b, i, 0)),
                   pl.BlockSpec((1, tm, cw), lambda b, i: (b, i, 0))],
        out_shape=[jax.ShapeDtypeStruct((bsz, s, n), BF16),
                   jax.ShapeDtypeStruct((bsz, s, cw), F32)],
        compiler_params=_cparams(("arbitrary", "arbitrary")),
        name="inproj",
    )(x, mod6, g.reshape(1, d), w_bf)


LOG2E = math.log2(math.e)


def _attn_kernel(lq_ref, q_ref, k_ref, v_ref, d_ref, g_ref, o_ref,
                 qs_ref, vt_ref, s_ref, m_ref, l_ref, acc_ref, *, t, lam_init):
    qi = pl.program_id(2)
    nkt = vt_ref.shape[0]
    slot = qi % 2
    blocks = [slice(c * ATTN_COLS, (c + 1) * ATTN_COLS) for c in range(2 * t // ATTN_COLS)]

    def prep(qidx, sl):
        q = q_ref[0, pl.ds(pl.multiple_of(qidx * t, t), t), :]
        lane = lax.broadcasted_iota(jnp.int32, q.shape, 1)
        qf = q.astype(F32) * (ATTN_HEAD_DIM ** -0.5 * LOG2E)
        zero = jnp.zeros_like(qf)
        qs_ref[sl, 0:t, :] = jnp.where(lane < ATTN_HEAD_DIM, qf, zero).astype(BF16)
        qs_ref[sl, t:2 * t, :] = jnp.where(lane >= ATTN_HEAD_DIM, qf, zero).astype(BF16)

    def scores(j, cs, sl):
        k = k_ref[0, pl.ds(pl.multiple_of(j * t, t), t), :]
        return lax.dot_general(k, qs_ref[sl, cs, :], (((1,), (1,)), ((), ())), preferred_element_type=F32)

    @pl.when(qi == 0)
    def _():
        for c in range(nkt):
            vt_ref[c] = v_ref[0, c * t:(c + 1) * t, :].T
        prep(0, 0)
        for cs in blocks:
            s_ref[:, cs] = scores(0, cs, 0)

    prep(jnp.minimum(qi + 1, pl.num_programs(2) - 1), 1 - slot)
    m_ref[...] = jnp.full(m_ref.shape, NEG, F32)
    l_ref[...] = jnp.zeros(l_ref.shape, F32)
    acc_ref[...] = jnp.zeros(acc_ref.shape, F32)

    def tile(j, bias_ref, j_next, next_slot=None):
        vt = vt_ref[j]
        m_all = m_ref[...]
        l_all = l_ref[...]
        ps, alphas, m_news, l_news = [], [], [], []
        for cs in blocks:
            s = s_ref[:, cs]
            if bias_ref is not None:
                s = s + bias_ref[:, cs]
            m_old = m_all[:, cs]
            m_new = jnp.maximum(m_old, jnp.max(s, axis=0, keepdims=True))
            alpha = jnp.exp2(m_old - m_new)
            p = jnp.exp2(s - m_new)
            l_news.append(alpha * l_all[:, cs] + jnp.sum(p, axis=0, keepdims=True))
            ps.append(p.astype(BF16))
            alphas.append(alpha)
            m_news.append(m_new)
            s_ref[:, cs] = scores(j_next, cs, slot if next_slot is None else next_slot)
        pvs = [jnp.dot(vt, p, preferred_element_type=F32) for p in ps]
        for cs, alpha, pv in zip(blocks, alphas, pvs):
            acc_ref[:, cs] = alpha * acc_ref[:, cs] + pv
        m_ref[...] = jnp.concatenate(m_news, axis=1)
        l_ref[...] = jnp.concatenate(l_news, axis=1)

    n_far = jnp.maximum(qi - 1, 0)

    def far_pair(jj, carry):
        tile(2 * jj, None, 2 * jj + 1)
        tile(2 * jj + 1, None, 2 * jj + 2)
        return carry

    lax.fori_loop(0, n_far // 2, far_pair, 0)

    @pl.when(n_far % 2 == 1)
    def _():
        tile(n_far - 1, None, n_far)

    @pl.when(qi >= 1)
    def _():
        tile(qi - 1, d_ref.at[0, 1], qi)

    tile(qi, d_ref.at[0, 0], 0, 1 - slot)

    lq = lq_ref[...]
    lam = (jnp.exp(jnp.sum(lq[0:1] * lq[1:2], axis=-1, keepdims=True))
           - jnp.exp(jnp.sum(lq[2:3] * lq[3:4], axis=-1, keepdims=True)) + lam_init)
    o = acc_ref[...] * (1.0 / l_ref[...])
    a = o[:, 0:t] - lam * o[:, t:2 * t]
    ms = jnp.mean(a * a, axis=0, keepdims=True)
    y = (a * lax.rsqrt(ms + EPS)).T * (g_ref[...] * (1.0 - lam_init))
    o_ref[0] = y.astype(o_ref.dtype)


def _t5_bucket(rel):
    n = jnp.maximum(rel, 0)
    max_exact = NUM_BUCKETS // 2
    nf = jnp.maximum(n, 1).astype(F32)
    large = max_exact + (jnp.log(nf / max_exact) / math.log(MAX_DISTANCE / max_exact)
                         * (NUM_BUCKETS - max_exact)).astype(jnp.int32)
    large = jnp.minimum(large, NUM_BUCKETS - 1)
    return jnp.where(n < max_exact, n, large)


def _bias_tiles(rel_table, t):
    assert t >= MAX_DISTANCE
    nh = rel_table.shape[1]
    rel = jnp.arange(2 * t)
    vec = (rel_table[_t5_bucket(rel)].astype(F32) - rel_table[NUM_BUCKETS - 1].astype(F32)[None, :]).T * LOG2E
    neg = jnp.full((nh, t), NEG, F32)

    def toeplitz(w):
        flat = jnp.tile(w, (1, t))[:, :t * (2 * t - 1)]
        return flat.reshape(nh, t, 2 * t - 1)[:, :, :t]

    diag = toeplitz(jnp.concatenate([vec[:, :t], neg], axis=1))
    prev = toeplitz(jnp.concatenate([vec[:, t:], vec[:, :t]], axis=1))
    both = jnp.stack([diag, prev], axis=1)
    return jnp.concatenate([both, both], axis=-1)


def _attention(proj, lq4, subln_g, rel_table, layer):
    bsz, s, _ = proj.shape
    t = min(T_ATTN, s)
    nq = s // t
    lam_init = 0.8 - 0.6 * math.exp(-0.3 * layer)
    dt = _bias_tiles(rel_table, t)
    hq, hk, hv = 0, ATTN_HEADS, 2 * ATTN_HEADS
    kern = functools.partial(_attn_kernel, t=t, lam_init=lam_init)
    return pl.pallas_call(
        kern,
        grid=(bsz, ATTN_HEADS, nq),
        in_specs=[pl.BlockSpec((4, ATTN_HEAD_DIM), lambda b, h, i: (0, 0)),
                  pl.BlockSpec((1, s, HEAD_W), lambda b, h, i: (b, 0, hq + h)),
                  pl.BlockSpec((1, s, HEAD_W), lambda b, h, i: (b, 0, hk + h)),
                  pl.BlockSpec((1, s, HEAD_W), lambda b, h, i: (b, 0, hv + h)),
                  pl.BlockSpec((1, 2, t, 2 * t), lambda b, h, i: (h, 0, 0, 0)),
                  pl.BlockSpec((1, HEAD_W), lambda b, h, i: (0, 0))],
        out_specs=pl.BlockSpec((1, t, HEAD_W), lambda b, h, i: (b, i, h)),
        out_shape=jax.ShapeDtypeStruct((bsz, s, ATTN_HEADS * HEAD_W), BF16),
        scratch_shapes=[pltpu.VMEM((2, 2 * t, HEAD_W), BF16),
                        pltpu.VMEM((nq, HEAD_W, t), BF16),
                        pltpu.VMEM((t, 2 * t), F32),
                        pltpu.VMEM((1, 2 * t), F32),
                        pltpu.VMEM((1, 2 * t), F32),
                        pltpu.VMEM((HEAD_W, 2 * t), F32)],
        compiler_params=_cparams(("arbitrary", "arbitrary", "arbitrary")),
        name="attn",
    )(lq4, proj, proj, proj, dt, subln_g.reshape(1, HEAD_W))


def _hgrn_kernel(lb_ref, tri_ref, q_ref, f_ref, i_ref, g_ref, ng_ref, o_ref, st_ref, *, tr, layer):
    c = REC_CHUNK
    nch = tr // c

    @pl.when(pl.program_id(2) == 0)
    def _():
        st_ref[...] = jnp.zeros(st_ref.shape, F32)

    lbr = lb_ref[...]
    e = jnp.exp(lbr - jnp.max(lbr, axis=0, keepdims=True))
    lb = jnp.sum(e[0:layer + 1], axis=0, keepdims=True) / jnp.sum(e, axis=0, keepdims=True)

    fr = f_ref[0]
    sg = _sigmoid(fr)
    lf = jnp.log2(lb + (1.0 - lb) * sg)
    kf = (1.0 - lb) * (1.0 - sg)
    qf = _silu(q_ref[0].astype(F32))
    vb = i_ref[0]

    row = lax.broadcasted_iota(jnp.int32, (tr, LANES), 0)

    lf_hi = lf.astype(BF16)
    lf_lo = (lf - lf_hi.astype(F32)).astype(BF16)
    lf2 = jnp.concatenate([lf_hi, lf_lo], axis=1)
    tw = tri_ref.shape[0]
    parts = []
    for r0 in range(0, tr, tw):
        cs2 = jnp.dot(tri_ref[...], lf2[r0:r0 + tw], preferred_element_type=F32)
        parts.append(cs2[:, :LANES] + cs2[:, LANES:])
    b = jnp.concatenate(parts, axis=0)

    kf3 = kf.reshape(tr // 8, 8, LANES)
    b3 = b.reshape(tr // 8, 8, LANES)
    diags = [jnp.sum(qf * kf, axis=-1, keepdims=True)]
    for dlt in range(1, 8):
        kd = pltpu.roll(kf3, dlt, axis=1).reshape(tr, LANES)
        bd = pltpu.roll(b3, dlt, axis=1).reshape(tr, LANES)
        term = qf * kd * jnp.exp2(b - bd)
        diags.append(jnp.sum(term, axis=-1, keepdims=True))
    lv_q, lv_k = [], []
    for w in (8, 16, 32):
        bend = jnp.broadcast_to(b.reshape(tr // w, w, LANES)[:, w - 1:w, :],
                                (tr // w, w, LANES)).reshape(tr, LANES)
        right = (row // w) % 2 == 1
        pend = pltpu.roll(bend, w, axis=0)
        lv_q.append(jnp.where(right, qf * jnp.exp2(jnp.minimum(b - pend, 0.0)), 0.0).astype(BF16))
        lv_k.append(jnp.where(right, 0.0, kf * jnp.exp2(jnp.minimum(bend - b, 0.0))).astype(BF16))

    blast = jnp.broadcast_to(b.reshape(nch, c, LANES)[:, c - 1:c, :], (nch, c, LANES)).reshape(tr, LANES)
    qdec = (qf * jnp.exp2(b)).astype(BF16)
    kdec = (kf * jnp.exp2(blast - b)).astype(BF16)
    dec = jnp.exp2(blast)

    ri = lax.broadcasted_iota(jnp.int32, (c, c), 0)
    ci = lax.broadcasted_iota(jnp.int32, (c, c), 1)
    nt = (((1,), (1,)), ((), ()))
    sls = [slice(ch * c, (ch + 1) * c) for ch in range(nch)]
    sws = [[lax.dot_general(lv_q[li][sl], lv_k[li][sl], nt, preferred_element_type=F32) for li in range(3)]
           for sl in sls]
    upds = [lax.dot_general(vb[sl], kdec[sl], (((0,), (0,)), ((), ())), preferred_element_type=F32)
            for sl in sls]
    sames = [ri // (2 * w) == ci // (2 * w) for w in (8, 16, 32)]
    dsel = [(ci == ri - dlt) & (ri % 8 >= dlt) for dlt in range(8)]
    a_bf = []
    for ch, sl in enumerate(sls):
        a = jnp.zeros((c, c), F32)
        for li in range(3):
            a = a + jnp.where(sames[li], sws[ch][li], 0.0)
        for dlt in range(8):
            a = a + jnp.where(dsel[dlt], diags[dlt][sl], 0.0)
        a_bf.append(a.astype(BF16))
    st = st_ref[...]
    sts = []
    for ch in range(nch):
        sts.append(st.astype(BF16))
        st = dec[ch * c:ch * c + 1, :] * st + upds[ch]
    st_ref[...] = st
    outs = [lax.dot_general(qdec[sl], sts[ch], nt, preferred_element_type=F32)
            + jnp.dot(a_bf[ch], vb[sl], preferred_element_type=F32) for ch, sl in enumerate(sls)]

    o = jnp.concatenate(outs, axis=0)
    ms = jnp.mean(o * o, axis=-1, keepdims=True)
    y = (o * lax.rsqrt(ms + EPS) * ng_ref[...]) * _silu(g_ref[0].astype(F32))
    o_ref[0] = y.astype(o_ref.dtype)


def _hgrn(proj, f32f, lb_raw, norm_g, layer):
    bsz, s, _ = proj.shape
    tr = min(T_REC, s)
    nl = lb_raw.shape[0]
    cq, ci, cg = 12, 20, 24
    kern = functools.partial(_hgrn_kernel, tr=tr, layer=layer)
    tw = min(REC_TRI, tr)
    r = jnp.arange(tw)
    tri = ((r[:, None] // REC_CHUNK == r[None, :] // REC_CHUNK) & (r[None, :] <= r[:, None])).astype(BF16)
    return pl.pallas_call(
        kern,
        grid=(bsz, REC_HEADS, s // tr),
        in_specs=[pl.BlockSpec((nl, HEAD_W), lambda b, h, i: (0, h)),
                  pl.BlockSpec((tw, tw), lambda b, h, i: (0, 0)),
                  pl.BlockSpec((1, tr, HEAD_W), lambda b, h, i: (b, i, cq + h)),
                  pl.BlockSpec((1, tr, HEAD_W), lambda b, h, i: (b, i, h)),
                  pl.BlockSpec((1, tr, HEAD_W), lambda b, h, i: (b, i, ci + h)),
                  pl.BlockSpec((1, tr, HEAD_W), lambda b, h, i: (b, i, cg + h)),
                  pl.BlockSpec((1, HEAD_W), lambda b, h, i: (0, h))],
        out_specs=pl.BlockSpec((1, tr, HEAD_W), lambda b, h, i: (b, i, h)),
        out_shape=jax.ShapeDtypeStruct((bsz, s, REC_HEADS * HEAD_W), BF16),
        scratch_shapes=[pltpu.VMEM((HEAD_W, HEAD_W), F32)],
        compiler_params=_cparams(("arbitrary", "arbitrary", "arbitrary")),
        name="hgrn",
    )(lb_raw, tri, proj, f32f, proj, proj, norm_g.reshape(1, -1))


def _pack_bf16_pairs(x):
    n = x.shape[1] // 2
    lo = pltpu.bitcast(x[:, :n].astype(BF16).astype(F32), jnp.uint32)
    hi = pltpu.bitcast(x[:, n:].astype(BF16).astype(F32), jnp.uint32)
    return (hi & jnp.uint32(0xFFFF0000)) | (lo >> 16)


def _join_bf16_pairs(lo, hi):
    return pltpu.bitcast(hi, jnp.uint32) | (pltpu.bitcast(lo, jnp.uint32) >> 16)


def _unpack_bf16_pairs(w):
    lo = pltpu.bitcast(w << 16, F32)
    hi = pltpu.bitcast(w & jnp.uint32(0xFFFF0000), F32)
    return lo, hi


def _outproj_kernel(ya_ref, yr_ref, x_ref, mod_ref, g_ref, wo_ref, wr_ref, br_ref,
                    x1_ref, h2_ref, meta_ref, metat_ref, cnt_ref, *, tm):
    half = ya_ref.shape[2]
    subs = [slice(r0, r0 + OUT_ROWS) for r0 in range(0, tm, OUT_ROWS)]
    lane = lax.broadcasted_iota(jnp.int32, (tm, LANES), 1)
    lane_f = lane.astype(F32)

    def first_lane(mask):
        return jnp.min(jnp.where(mask, lane_f, float(LANES)), axis=-1, keepdims=True).astype(jnp.int32)

    def route(logits):
        is_g = lane < N_GROUPS
        gl = jnp.where(is_g, logits, NEG)
        gmax = jnp.max(gl, axis=-1, keepdims=True)
        gsum = jnp.sum(jnp.exp(gl - gmax), axis=-1, keepdims=True)
        g_gate = 1.0 / gsum
        g_idx = first_lane(is_g & (gl == gmax))
        e_lo = N_GROUPS + g_idx * EXPERTS_PER_GROUP
        in_grp = (lane >= e_lo) & (lane < e_lo + EXPERTS_PER_GROUP)
        el = jnp.where(in_grp, logits, NEG)
        m1 = jnp.max(el, axis=-1, keepdims=True)
        i1 = first_lane(in_grp & (el == m1))
        el2 = jnp.where(lane == i1, NEG, el)
        m2 = jnp.max(el2, axis=-1, keepdims=True)
        i2 = first_lane(in_grp & (el2 == m2))
        r = jnp.exp(m2 - m1)
        return i1 - N_GROUPS, i2 - N_GROUPS, g_gate / (1.0 + r), g_gate * r / (1.0 + r)

    mixes = [jnp.dot(ya_ref[0, rs, :], wo_ref[0:half, :], preferred_element_type=F32)
             + jnp.dot(yr_ref[0, rs, :], wo_ref[half:2 * half, :], preferred_element_type=F32) for rs in subs]
    logit_l = []
    for rs, mix in zip(subs, mixes):
        x1 = x_ref[0, rs, :] + mod_ref[0, 2:3, :] * mix
        x1_ref[0, rs, :] = x1
        h2 = _norm_mod(x1, g_ref[...], mod_ref[0, 4:5, :], mod_ref[0, 3:4, :])
        h2_ref[0, rs, :] = _pack_bf16_pairs(h2)
        logit_l.append(jnp.dot(h2.astype(BF16), wr_ref[...], preferred_element_type=F32) + br_ref[...])
    e1, e2, w1, w2 = route(jnp.concatenate(logit_l, axis=0))

    oh1 = (lane == e1).astype(F32)
    oh2 = (lane == e2).astype(F32)
    oh = oh1 + oh2
    ri = lax.broadcasted_iota(jnp.int32, (tm, tm), 0)
    ci = lax.broadcasted_iota(jnp.int32, (tm, tm), 1)
    tri = (ci < ri).astype(BF16)
    rank = jnp.dot(tri, oh.astype(BF16), preferred_element_type=F32)
    cnt = jnp.sum(oh, axis=0, keepdims=True)
    cnt_up = jnp.floor((cnt + (GROUP - 1.0)) * (1.0 / GROUP)) * GROUP
    li = lax.broadcasted_iota(jnp.int32, (LANES, LANES), 0)
    lj = lax.broadcasted_iota(jnp.int32, (LANES, LANES), 1)
    before = (li < lj).astype(BF16)
    lstart = jnp.dot(jnp.broadcast_to(cnt_up, (8, LANES)).astype(BF16), before,
                     preferred_element_type=F32)[0:1, :]
    pos = rank + lstart
    p1 = jnp.sum(pos * oh1, axis=-1, keepdims=True)
    p2 = jnp.sum(pos * oh2, axis=-1, keepdims=True)
    vals = (p1, p2, w1, w2)
    meta = jnp.zeros((tm, LANES), F32)
    for i, v in enumerate(vals):
        meta = jnp.where(lane == i, v, meta)
    meta_ref[0] = meta
    metat_ref[0] = meta.T[0:8, :]
    cnt_ref[0] = jnp.broadcast_to(cnt, (8, LANES))


def _outproj(ya, yr, x, mod6, g2n, wo_bf, wr_bf, br):
    bsz, s, d = x.shape
    tm = min(TM_PROJ, s)
    half = ya.shape[2]
    kern = functools.partial(_outproj_kernel, tm=tm)
    tok = lambda b, i: (b, i, 0)
    nt = s // tm
    tile = lambda b, i: (b * nt + i, 0, 0)
    return pl.pallas_call(
        kern,
        grid=(bsz, nt),
        in_specs=[pl.BlockSpec((1, tm, half), tok),
                  pl.BlockSpec((1, tm, half), tok),
                  pl.BlockSpec((1, tm, d), tok),
                  pl.BlockSpec((1, 6, d), lambda b, i: (b, 0, 0)),
                  pl.BlockSpec((1, d), lambda b, i: (0, 0)),
                  pl.BlockSpec((2 * half, d), lambda b, i: (0, 0)),
                  pl.BlockSpec((d, LANES), lambda b, i: (0, 0)),
                  pl.BlockSpec((1, LANES), lambda b, i: (0, 0))],
        out_specs=[pl.BlockSpec((1, tm, d), tok),
                   pl.BlockSpec((1, tm, d // 2), tok),
                   pl.BlockSpec((1, tm, LANES), tok),
                   pl.BlockSpec((1, 8, tm), tile),
                   pl.BlockSpec((1, 8, LANES), tile)],
        out_shape=[jax.ShapeDtypeStruct((bsz, s, d), F32),
                   jax.ShapeDtypeStruct((bsz, s, d // 2), jnp.uint32),
                   jax.ShapeDtypeStruct((bsz, s, LANES), F32),
                   jax.ShapeDtypeStruct((bsz * nt, 8, tm), F32),
                   jax.ShapeDtypeStruct((bsz * nt, 8, LANES), F32)],
        compiler_params=_cparams(("arbitrary", "arbitrary")),
        name="outproj",
    )(ya, yr, x, mod6, g2n.reshape(1, d), wo_bf, wr_bf, br)


GROUP = 8
DMA_UNROLL = 8


def _local_groups(tm):
    need = (tm * TOP_K + N_EXPERTS * (GROUP - 1)) // GROUP
    return -(-need // DMA_UNROLL) * DMA_UNROLL


def _dispatch_kernel(gdst_ref, metat_ref, h_ref, rows_ref, sort_ref, sem, *, tm, ng):
    tile = pl.program_id(0)
    last = pl.num_programs(0) - 1
    slot = tile % 2

    def drain(s):
        pltpu.make_async_copy(sort_ref.at[s], rows_ref.at[pl.ds(0, ng)], sem.at[s]).wait()

    @pl.when(tile >= 2)
    def _():
        drain(slot)

    p1 = metat_ref[0, 0:1, :].astype(jnp.int32)
    p2 = metat_ref[0, 1:2, :].astype(jnp.int32)
    r = lax.broadcasted_iota(jnp.int32, (ng * GROUP, tm), 0)
    perm = ((r == p1) | (r == p2)).astype(BF16)
    lo, hi = _unpack_bf16_pairs(h_ref[...])
    slo = jnp.dot(perm, lo.astype(BF16), preferred_element_type=F32)
    shi = jnp.dot(perm, hi.astype(BF16), preferred_element_type=F32)
    sort_ref[slot] = _join_bf16_pairs(slo, shi).reshape(sort_ref.shape[1:])

    def start(g, carry):
        dst = gdst_ref[tile * ng + g]
        pltpu.make_async_copy(sort_ref.at[slot, g], rows_ref.at[dst], sem.at[slot]).start()
        return carry

    lax.fori_loop(0, ng, start, 0, unroll=DMA_UNROLL)

    @pl.when(tile == last)
    def _():
        @pl.when(tile >= 1)
        def _():
            drain(1 - slot)

        drain(slot)


def _dispatch(gdst, meta_t, h2p):
    n, dw = h2p.shape
    tm = meta_t.shape[2]
    ng = _local_groups(tm)
    kern = functools.partial(_dispatch_kernel, tm=tm, ng=ng)
    return pl.pallas_call(
        kern,
        grid_spec=pltpu.PrefetchScalarGridSpec(
            num_scalar_prefetch=1,
            grid=(n // tm,),
            in_specs=[pl.BlockSpec((1, 8, tm), lambda i, a: (i, 0, 0)),
                      pl.BlockSpec((tm, dw), lambda i, a: (i, 0))],
            out_specs=pl.BlockSpec(memory_space=pl.ANY),
            scratch_shapes=[pltpu.VMEM((2, ng, GROUP, dw), jnp.uint32),
                            pltpu.SemaphoreType.DMA((2,))]),
        out_shape=jax.ShapeDtypeStruct((n // tm * ng, GROUP, dw), jnp.uint32),
        compiler_params=_cparams(("arbitrary",)),
        name="dispatch",
    )(gdst, meta_t, h2p)


def _group_tables(tcnt, tm):
    ng = _local_groups(tm)
    g8 = (tcnt + (GROUP - 1)) // GROUP
    gcount = jnp.sum(g8, axis=0)
    gstart = jnp.cumsum(gcount) - gcount
    gtile = gstart[None, :] + jnp.cumsum(g8, axis=0) - g8
    lend = jnp.cumsum(g8, axis=1)
    lstart = lend - g8
    g = jnp.arange(ng, dtype=jnp.int32)
    exp_of = jnp.sum(lend[:, None, :] <= g[None, :, None], axis=-1)
    onehot = exp_of[..., None] == jnp.arange(N_EXPERTS, dtype=jnp.int32)
    base = jnp.sum(jnp.where(onehot, (gtile - lstart)[:, None, :], 0), axis=-1)
    n_used = lend[:, -1:]
    used = g[None, :] < n_used
    n_spare = ng - n_used
    spare0 = jnp.sum(gcount) + jnp.cumsum(n_spare, axis=0) - n_spare
    gdst = jnp.where(used, base + g[None, :], spare0 + g[None, :] - n_used).astype(jnp.int32).reshape(-1)
    gsrc = jnp.where(used, base + g[None, :], 0).astype(jnp.int32).reshape(-1)
    return gdst, gsrc, gcount * GROUP


def _expert_kernel(wblk_ref, wexp_ref, wlo_ref, whi_ref, rows_ref, w1_ref, w3_ref, w2_ref, y_ref,
                   w1b_ref, w3b_ref, w2b_ref, acc_ref, *, blk, n_work):
    w = pl.program_id(0)
    prev = jnp.maximum(w - 1, 0)
    nxt = jnp.minimum(w + 1, n_work - 1)
    new_exp = (w == 0) | (wexp_ref[w] != wexp_ref[prev])
    new_blk = (w == 0) | (wblk_ref[w] != wblk_ref[prev])
    last_of_blk = (w == n_work - 1) | (wblk_ref[w] != wblk_ref[nxt])
    lo = wlo_ref[w]
    hi = whi_ref[w]

    @pl.when(new_exp)
    def _():
        w1b_ref[...] = w1_ref[0].astype(BF16)
        w3b_ref[...] = w3_ref[0].astype(BF16)
        w2b_ref[...] = w2_ref[0].astype(BF16)

    full = (lo == 0) & (hi == blk)

    @pl.when(new_blk & jnp.logical_not(full))
    def _():
        acc_ref[...] = jnp.zeros(acc_ref.shape, F32)

    def ffn():
        xlo, xhi = _unpack_bf16_pairs(rows_ref[...])
        half = xlo.shape[1]
        xlo = xlo.astype(BF16)
        xhi = xhi.astype(BF16)
        a = (jnp.dot(xlo, w1b_ref[0:half, :], preferred_element_type=F32)
             + jnp.dot(xhi, w1b_ref[half:2 * half, :], preferred_element_type=F32))
        b = (jnp.dot(xlo, w3b_ref[0:half, :], preferred_element_type=F32)
             + jnp.dot(xhi, w3b_ref[half:2 * half, :], preferred_element_type=F32))
        hmid = (_silu(a) * b).astype(BF16)
        return jnp.dot(hmid, w2b_ref[...], preferred_element_type=F32)

    @pl.when(full)
    def _():
        acc_ref[...] = ffn()

    @pl.when((hi > lo) & jnp.logical_not(full))
    def _():
        row = lax.broadcasted_iota(jnp.int32, (blk, 1), 0)
        keep = (row >= lo) & (row < hi)
        acc_ref[...] = acc_ref[...] + jnp.where(keep, ffn(), 0.0)

    @pl.when(last_of_blk)
    def _():
        y_ref[...] = _pack_bf16_pairs(acc_ref[...])


def _experts(wblk, wexp, wlo, whi, rows, a_rows, w1, w3, w2):
    dw = rows.shape[1]
    n_work = wblk.shape[0]
    blk = math.gcd(BLK_E, a_rows)
    _, d, ff = w1.shape
    kern = functools.partial(_expert_kernel, blk=blk, n_work=n_work)
    return pl.pallas_call(
        kern,
        grid_spec=pltpu.PrefetchScalarGridSpec(
            num_scalar_prefetch=4,
            grid=(n_work,),
            in_specs=[pl.BlockSpec((blk, dw), lambda w, wb, we, wl, wh: (wb[w], 0)),
                      pl.BlockSpec((1, d, ff), lambda w, wb, we, wl, wh: (we[w], 0, 0)),
                      pl.BlockSpec((1, d, ff), lambda w, wb, we, wl, wh: (we[w], 0, 0)),
                      pl.BlockSpec((1, ff, d), lambda w, wb, we, wl, wh: (we[w], 0, 0))],
            out_specs=pl.BlockSpec((blk, dw), lambda w, wb, we, wl, wh: (wb[w], 0)),
            scratch_shapes=[pltpu.VMEM((d, ff), BF16),
                            pltpu.VMEM((d, ff), BF16),
                            pltpu.VMEM((ff, d), BF16),
                            pltpu.VMEM((blk, d), F32)]),
        out_shape=jax.ShapeDtypeStruct((a_rows, dw), jnp.uint32),
        compiler_params=_cparams(("arbitrary",)),
        name="experts",
    )(wblk, wexp, wlo, whi, rows, w1, w3, w2)


def _work_items(counts, a_rows, blk):
    n_blk = a_rows // blk
    ends = jnp.cumsum(counts)
    starts = ends - counts
    total = ends[-1]
    pts = jnp.sort(jnp.concatenate([jnp.arange(n_blk, dtype=jnp.int32) * blk, starts[1:]]))
    nxt = jnp.concatenate([pts[1:], jnp.array([a_rows], jnp.int32)])
    blk_of = jnp.minimum(pts // blk, n_blk - 1)
    wexp = jnp.minimum(jnp.sum(ends[None, :] <= pts[:, None], axis=1), N_EXPERTS - 1).astype(jnp.int32)
    lo = pts - blk_of * blk
    hi = jnp.minimum(jnp.minimum(nxt, total), (blk_of + 1) * blk) - blk_of * blk
    hi = jnp.maximum(hi, lo)
    return blk_of.astype(jnp.int32), wexp, lo.astype(jnp.int32), hi.astype(jnp.int32)


def _combine_kernel(gsrc_ref, y_ref, x1_ref, meta_ref, mod_ref, g_ref, o_ref, sort_ref, sem, *, tm, ng):
    tile = pl.program_id(0) * pl.num_programs(1) + pl.program_id(1)
    n_tiles = pl.num_programs(0) * pl.num_programs(1)
    slot = tile % 2

    def fetch(t, s):
        def start(g, carry):
            src = gsrc_ref[t * ng + g]
            pltpu.make_async_copy(y_ref.at[src], sort_ref.at[s, g], sem.at[s]).start()
            return carry

        lax.fori_loop(0, ng, start, 0, unroll=DMA_UNROLL)

    @pl.when(tile == 0)
    def _():
        fetch(tile, slot)

    @pl.when(tile + 1 < n_tiles)
    def _():
        fetch(tile + 1, 1 - slot)

    pltpu.make_async_copy(y_ref.at[pl.ds(0, ng)], sort_ref.at[slot], sem.at[slot]).wait()

    meta = meta_ref[0]
    nrow = ng * GROUP
    lo, hi = _unpack_bf16_pairs(sort_ref[slot].reshape(nrow, sort_ref.shape[3]))
    lo = lo.astype(BF16)
    hi = hi.astype(BF16)
    col = lax.broadcasted_iota(jnp.int32, (tm, nrow), 1)
    mix = jnp.zeros((tm, nrow), F32)
    for kk in range(TOP_K):
        mix = jnp.where(col == meta[:, kk:kk + 1].astype(jnp.int32), meta[:, TOP_K + kk:TOP_K + kk + 1], mix)
    mix = mix.astype(BF16)
    moe = jnp.concatenate([jnp.dot(mix, lo, preferred_element_type=F32),
                           jnp.dot(mix, hi, preferred_element_type=F32)], axis=1)
    x2 = x1_ref[0] + mod_ref[0, 5:6, :] * moe
    ms = jnp.mean(x2 * x2, axis=-1, keepdims=True)
    o_ref[0] = x2 * lax.rsqrt(ms + EPS) * g_ref[...]


def _combine(gsrc, ybuf3, x1, meta, mod6, gf):
    bsz, s, d = x1.shape
    tm = min(TM_PROJ, s)
    ng = _local_groups(tm)
    kern = functools.partial(_combine_kernel, tm=tm, ng=ng)
    tok = lambda b, i, a: (b, i, 0)
    return pl.pallas_call(
        kern,
        grid_spec=pltpu.PrefetchScalarGridSpec(
            num_scalar_prefetch=1,
            grid=(bsz, s // tm),
            in_specs=[pl.BlockSpec(memory_space=pl.ANY),
                      pl.BlockSpec((1, tm, d), tok),
                      pl.BlockSpec((1, tm, LANES), tok),
                      pl.BlockSpec((1, 6, d), lambda b, i, a: (b, 0, 0)),
                      pl.BlockSpec((1, d), lambda b, i, a: (0, 0))],
            out_specs=pl.BlockSpec((1, tm, d), tok),
            scratch_shapes=[pltpu.VMEM((2, ng, GROUP, d // 2), jnp.uint32),
                            pltpu.SemaphoreType.DMA((2,))]),
        out_shape=jax.ShapeDtypeStruct((bsz, s, d), F32),
        compiler_params=_cparams(("arbitrary", "arbitrary")),
        name="combine",
    )(gsrc, ybuf3, x1, meta, mod6, gf.reshape(1, d))


def kernel(x, c, w_ada, b_ada, norm1_g, norm2_g, w_in, attn_lambda_q1, attn_lambda_k1, attn_lambda_q2,
           attn_lambda_k2, attn_subln_g, rel_bias_table, rec_lower_bound, rec_norm_g, w_out, w_group,
           b_group, w_expert, b_expert, w1, w3, w2, final_norm_g):
    bsz, s, d = x.shape
    depth = w_ada.shape[0]
    n_tok = bsz * s
    for l in range(depth):
        mod6 = _mod(c, w_ada[l], b_ada[l]).reshape(bsz, 6, d)
        proj, f32f = _inproj(x, mod6, norm1_g[l], w_in[l].astype(BF16))
        lq4 = jnp.stack([attn_lambda_q1[l], attn_lambda_k1[l], attn_lambda_q2[l], attn_lambda_k2[l]])
        ya = _attention(proj, lq4, attn_subln_g[l], rel_bias_table, l)
        yr = _hgrn(proj, f32f, rec_lower_bound, rec_norm_g[l], l)
        w_r = jnp.zeros((d, LANES), F32).at[:, :N_GROUPS].set(w_group[l])
        w_r = w_r.at[:, N_GROUPS:N_GROUPS + N_EXPERTS].set(w_expert[l]).astype(BF16)
        b_r = jnp.zeros((1, LANES), F32).at[0, :N_GROUPS].set(b_group[l])
        b_r = b_r.at[0, N_GROUPS:N_GROUPS + N_EXPERTS].set(b_expert[l])
        x1, h2p, meta, meta_t, cnt = _outproj(ya, yr, x, mod6, norm2_g[l], w_out[l].astype(BF16), w_r, b_r)
        tm = meta_t.shape[2]
        tcnt = cnt[:, 0, :N_EXPERTS].astype(jnp.int32)
        gdst, gsrc, counts_up = _group_tables(tcnt, tm)
        rows3 = _dispatch(gdst, meta_t, h2p.reshape(n_tok, d // 2))
        buf_rows = rows3.shape[0] * GROUP
        blk = math.gcd(BLK_E, buf_rows)
        wblk, wexp, wlo, whi = _work_items(counts_up, buf_rows, blk)
        ybuf = _experts(wblk, wexp, wlo, whi, rows3.reshape(buf_rows, d // 2), buf_rows, w1[l], w3[l], w2[l])
        gf = final_norm_g if l == depth - 1 else jnp.ones((d,), F32)
        assert depth == 1
        x = _combine(gsrc, ybuf.reshape(buf_rows // GROUP, GROUP, d // 2), x1, meta, mod6, gf)
    return x
```
